```python
import jax, jax.numpy as jnp
from jax import lax
import numpy as np

D_MODEL = 2048
BATCH = 32
SEQ = 256
DEPTH = 1
DEC_BATCH = 8
DEC_SEQ = 4096
PAST_LEN = 256

GRID_W = 64
HEAD_DIM = 128
N_HEADS = 8
N_KV_HEADS = 2
GROUPS = N_HEADS // N_KV_HEADS
Q_WIDTH = N_HEADS * HEAD_DIM
KV_WIDTH = N_KV_HEADS * HEAD_DIM
CONV_WIDTH = D_MODEL // 2
CONV_K = 3
D_FF = 4 * D_MODEL
Q_BLOCK = 128
ROPE_THETA = 10000.0
EPS = 1e-6
N_MOD = 6
IN_SIZES = (Q_WIDTH, KV_WIDTH, KV_WIDTH, CONV_WIDTH, CONV_WIDTH, CONV_WIDTH, D_MODEL, D_MODEL)
IN_WIDTH = sum(IN_SIZES)
IN_SPLITS = tuple(sum(IN_SIZES[:i + 1]) for i in range(len(IN_SIZES) - 1))

kernel_name = "hybrid_prefix_dit_gqa_shortconv_step"


def _rmsnorm(x, g):
    xf = x.astype(jnp.float32)
    y = xf * lax.rsqrt(jnp.mean(xf * xf, axis=-1, keepdims=True) + EPS)
    return (y * g.astype(jnp.float32)).astype(x.dtype)


def _axial_rope_tables(n_tokens, dtype):
    rows = n_tokens // GRID_W
    row = jnp.broadcast_to(jnp.arange(rows)[:, None], (rows, GRID_W)).reshape(-1).astype(jnp.float32)
    col = jnp.broadcast_to(jnp.arange(GRID_W)[None, :], (rows, GRID_W)).reshape(-1).astype(jnp.float32)
    half = HEAD_DIM // 2
    inv = ROPE_THETA ** (-jnp.arange(0, half, 2, dtype=jnp.float32) / half)
    ang = jnp.concatenate([row[:, None] * inv, col[:, None] * inv], axis=-1)
    return jnp.cos(ang).astype(dtype), jnp.sin(ang).astype(dtype)


def _rope(x, cos, sin):
    b, t, h, d = x.shape
    xp = x.reshape(b, t, h, d // 2, 2)
    x0, x1 = xp[..., 0], xp[..., 1]
    cs = cos[None, :, None, :]
    sn = sin[None, :, None, :]
    return jnp.stack([x0 * cs - x1 * sn, x0 * sn + x1 * cs], axis=-1).reshape(b, t, h, d)


def _short_conv(u, w):
    up = jnp.pad(u, ((0, 0), (1, 1), (0, 0)))
    return up[:, :-2] * w[0] + up[:, 1:-1] * w[1] + up[:, 2:] * w[2]


def _attention(q, k, v):
    b, t = q.shape[0], q.shape[1]
    nb = t // Q_BLOCK
    scale = HEAD_DIM ** -0.5
    qb = q.reshape(b, nb, Q_BLOCK, N_KV_HEADS, GROUPS, HEAD_DIM).transpose(1, 0, 2, 3, 4, 5)

    def block(qblk):
        s = jnp.einsum('bqkgd,bskd->bkgqs', qblk, k).astype(jnp.float32) * scale
        p = jax.nn.softmax(s, axis=-1).astype(v.dtype)
        return jnp.einsum('bkgqs,bskd->bqkgd', p, v)

    out = lax.map(block, qb)
    return out.transpose(1, 0, 2, 3, 4, 5).reshape(b, t, N_HEADS * HEAD_DIM)


def _modulation(cond, w_mod, b_mod):
    m = (jax.nn.silu(cond) @ w_mod + b_mod)[:, None, :]
    return jnp.split(m, N_MOD, axis=-1)


def _layer(x, mods, rope, ctx_kv, norm1_g, norm2_g, w_in, q_gain, k_gain, conv_w,
           w_attn_out, w_conv_out, w_o, w_ff1, w_ff2):
    b, t, _ = x.shape
    sh1, sc1, g1, sh2, sc2, g2 = mods
    h = _rmsnorm(x, norm1_g) * (1 + sc1) + sh1
    z = h @ w_in
    q, k, v, u, gate_c, gate_b, br_attn, br_conv = jnp.split(z, IN_SPLITS, axis=-1)
    q = _rmsnorm(q.reshape(b, t, N_HEADS, HEAD_DIM), q_gain)
    k = _rmsnorm(k.reshape(b, t, N_KV_HEADS, HEAD_DIM), k_gain)
    v = v.reshape(b, t, N_KV_HEADS, HEAD_DIM)
    own_kv = (k, v)
    if rope is not None:
        q = _rope(q, *rope)
        k = _rope(k, *rope)
    if ctx_kv is not None:
        k = jnp.concatenate([k, ctx_kv[0]], axis=1)
        v = jnp.concatenate([v, ctx_kv[1]], axis=1)
    attn = _attention(q, k, v)
    conv = gate_b * _short_conv(gate_c * u, conv_w)
    merged = jax.nn.sigmoid(br_attn) * (attn @ w_attn_out) + jax.nn.sigmoid(br_conv) * (conv @ w_conv_out)
    x = x + g1 * (merged @ w_o)
    h2 = _rmsnorm(x, norm2_g) * (1 + sc2) + sh2
    f = jnp.square(jax.nn.relu(h2 @ w_ff1)) @ w_ff2
    x = x + g2 * f
    return x, own_kv


def setup_inputs(seed: int = 0) -> dict:
    key = jax.random.key(seed)
    ks = jax.random.split(key, 24)

    def nrm(k, shape, fan_in):
        return jax.random.normal(k, shape, jnp.float32) * (fan_in ** -0.5)

    def gain(k, shape):
        return 1.0 + 0.01 * jax.random.normal(k, shape, jnp.float32)

    return {
        "x_prompt": jax.random.normal(ks[0], (BATCH, SEQ, D_MODEL), jnp.float32),
        "x_sample": jax.random.normal(ks[1], (DEC_BATCH, DEC_SEQ, D_MODEL), jnp.float32),
        "cache_k": jax.random.normal(ks[2], (DEC_BATCH, DEPTH, PAST_LEN, N_KV_HEADS, HEAD_DIM), jnp.float32),
        "cache_v": jax.random.normal(ks[3], (DEC_BATCH, DEPTH, PAST_LEN, N_KV_HEADS, HEAD_DIM), jnp.float32),
        "c": jax.random.normal(ks[4], (DEC_BATCH, D_MODEL), jnp.float32),
        "c_ctx": jax.random.normal(ks[5], (D_MODEL,), jnp.float32),
        "w_mod": nrm(ks[6], (DEPTH, D_MODEL, N_MOD * D_MODEL), D_MODEL),
        "b_mod": 0.02 * jax.random.normal(ks[7], (DEPTH, N_MOD * D_MODEL), jnp.float32),
        "norm1_g": gain(ks[8], (DEPTH, D_MODEL)),
        "norm2_g": gain(ks[9], (DEPTH, D_MODEL)),
        "w_in": nrm(ks[10], (DEPTH, D_MODEL, IN_WIDTH), D_MODEL),
        "q_gain": gain(ks[11], (DEPTH, HEAD_DIM)),
        "k_gain": gain(ks[12], (DEPTH, HEAD_DIM)),
        "conv_w": nrm(ks[13], (DEPTH, CONV_K, CONV_WIDTH), CONV_K),
        "w_attn_out": nrm(ks[14], (DEPTH, Q_WIDTH, D_MODEL), Q_WIDTH),
        "w_conv_out": nrm(ks[15], (DEPTH, CONV_WIDTH, D_MODEL), CONV_WIDTH),
        "w_o": nrm(ks[16], (DEPTH, D_MODEL, D_MODEL), D_MODEL),
        "w_ff1": nrm(ks[17], (DEPTH, D_MODEL, D_FF), D_MODEL),
        "w_ff2": nrm(ks[18], (DEPTH, D_FF, D_MODEL), D_FF),
        "final_g": gain(ks[19], (D_MODEL,)),
    }


def reference(x_prompt, x_sample, cache_k, cache_v, c, c_ctx, w_mod, b_mod, norm1_g, norm2_g,
              w_in, q_gain, k_gain, conv_w, w_attn_out, w_conv_out, w_o, w_ff1, w_ff2, final_g):
    rope = _axial_rope_tables(x_sample.shape[1], x_sample.dtype)
    xp = x_prompt
    xs = x_sample
    new_ks = []
    new_vs = []
    for l in range(DEPTH):
        weights = (norm1_g[l], norm2_g[l], w_in[l], q_gain[l], k_gain[l], conv_w[l],
                   w_attn_out[l], w_conv_out[l], w_o[l], w_ff1[l], w_ff2[l])
        mods_ctx = _modulation(c_ctx[None, :], w_mod[l], b_mod[l])
        xp, (k_ctx, v_ctx) = _layer(xp, mods_ctx, None, None, *weights)
        new_ks.append(k_ctx)
        new_vs.append(v_ctx)
        mods_lat = _modulation(c, w_mod[l], b_mod[l])
        xs, _ = _layer(xs, mods_lat, rope, (cache_k[:, l], cache_v[:, l]), *weights)
    y_prompt = _rmsnorm(xp, final_g)
    y_sample = _rmsnorm(xs, final_g)
    new_k = jnp.stack(new_ks, axis=1)
    new_v = jnp.stack(new_vs, axis=1)
    return (y_prompt, y_sample, new_k, new_v)
```

```python
import functools

import jax
import jax.numpy as jnp
from jax import lax
from jax.experimental import pallas as pl
from jax.experimental.pallas import tpu as pltpu

D_MODEL = 2048
GRID_W = 64
HEAD_DIM = 128
N_HEADS = 8
N_KV_HEADS = 2
GROUPS = N_HEADS // N_KV_HEADS
Q_WIDTH = N_HEADS * HEAD_DIM
KV_WIDTH = N_KV_HEADS * HEAD_DIM
CONV_WIDTH = D_MODEL // 2
D_FF = 4 * D_MODEL
ROPE_THETA = 10000.0
EPS = 1e-6
N_MOD = 6
COND_ROWS = 16

OFF_Q = 0
OFF_K = Q_WIDTH
OFF_V = OFF_K + KV_WIDTH
OFF_U = OFF_V + KV_WIDTH
OFF_C = OFF_U + CONV_WIDTH
OFF_B = OFF_C + CONV_WIDTH
OFF_A = OFF_B + CONV_WIDTH
OFF_S = OFF_A + D_MODEL
IN_WIDTH = OFF_S + D_MODEL
PROJ_WIDTH = OFF_A

MIB = 1024 * 1024
F32 = jnp.float32
BF16 = jnp.bfloat16


def _params(sem, vmem_mib):
    return pltpu.CompilerParams(dimension_semantics=sem, vmem_limit_bytes=vmem_mib * MIB)


def _mod_spec(chunk, row_fn):
    return pl.BlockSpec((None, 1, D_MODEL), lambda i, *_: (row_fn(i), 0, chunk))


def _modulated_norm(x, g, sc, sh):
    ms = jnp.mean(x * x, axis=-1, keepdims=True)
    y = x * lax.rsqrt(ms + EPS) * g
    return y * (1.0 + sc) + sh


def _head_norm(z, gain):
    ms = jnp.mean(z * z, axis=-1, keepdims=True)
    return z * lax.rsqrt(ms + EPS) * gain


def _rope(x, cos2, sin2):
    lane = lax.broadcasted_iota(jnp.int32, x.shape, 1)
    nxt = pltpu.roll(x, HEAD_DIM - 1, axis=1)
    prv = pltpu.roll(x, 1, axis=1)
    swapped = jnp.where(lane % 2 == 0, nxt, prv)
    return x * cos2 + swapped * sin2


def _mod_kernel(cond_ref, w_ref, b_ref, o_ref):
    cond = cond_ref[...]
    a = (cond * jax.nn.sigmoid(cond)).astype(BF16)
    o_ref[...] = jnp.dot(a, w_ref[...].astype(BF16), preferred_element_type=F32) + b_ref[...]


def _modulation(cond, w_mod, b_mod):
    tn = 1024
    n = w_mod.shape[1]
    return pl.pallas_call(
        _mod_kernel,
        grid=(n // tn,),
        in_specs=[
            pl.BlockSpec((COND_ROWS, D_MODEL), lambda j: (0, 0)),
            pl.BlockSpec((D_MODEL, tn), lambda j: (0, j)),
            pl.BlockSpec((1, tn), lambda j: (0, j)),
        ],
        out_specs=pl.BlockSpec((COND_ROWS, tn), lambda j: (0, j)),
        out_shape=jax.ShapeDtypeStruct((COND_ROWS, n), F32),
        compiler_params=_params(("arbitrary",), 40),
        name="modulation",
    )(cond, w_mod, b_mod.reshape(1, n))


def _inproj_kernel(*refs, rope, emit_kv):
    x_ref, sh_ref, sc_ref, g_ref, w_ref, qg_ref, kg_ref = refs[:7]
    pos = 7
    if rope:
        cos_ref, sin_ref = refs[pos:pos + 2]
        pos += 2
    q_ref, k_ref, v_ref = refs[pos:pos + 3]
    pos += 3
    if emit_kv:
        kf_ref, vf_ref = refs[pos:pos + 2]
        pos += 2
    cu_ref, bg_ref, h_ref = refs[pos:pos + 3]

    h_ref[...] = _modulated_norm(x_ref[...], g_ref[...], sc_ref[...], sh_ref[...]).astype(BF16)

    def proj(off, width):
        return jnp.dot(h_ref[...], w_ref[:, off:off + width], preferred_element_type=F32)

    if rope:
        cos2 = cos_ref[...]
        sin2 = sin_ref[...]
    scale = HEAD_DIM ** -0.5

    for hd in range(N_HEADS // 4):
        z = proj(OFF_Q + hd * 4 * HEAD_DIM, 4 * HEAD_DIM)
        for hh in range(4):
            qn = _head_norm(z[:, hh * HEAD_DIM:(hh + 1) * HEAD_DIM], qg_ref[...])
            if rope:
                qn = _rope(qn, cos2, sin2)
            col = (hd * 4 + hh) * HEAD_DIM
            q_ref[:, col:col + HEAD_DIM] = (qn * scale).astype(BF16)

    z = proj(OFF_K, 2 * KV_WIDTH)
    for hh in range(N_KV_HEADS):
        col = hh * HEAD_DIM
        kn = _head_norm(z[:, col:col + HEAD_DIM], kg_ref[...])
        if emit_kv:
            kf_ref[:, col:col + HEAD_DIM] = kn
        if rope:
            kn = _rope(kn, cos2, sin2)
        k_ref[:, col:col + HEAD_DIM] = kn.astype(BF16)
    vv = z[:, KV_WIDTH:]
    if emit_kv:
        vf_ref[...] = vv
    v_ref[...] = vv.astype(BF16)

    cw = 256
    for c in range(CONV_WIDTH // cw):
        u = proj(OFF_U + c * cw, cw)
        gc = proj(OFF_C + c * cw, cw)
        cu_ref[:, c * cw:(c + 1) * cw] = gc * u

    bw = 512
    for c in range(CONV_WIDTH // bw):
        bg_ref[:, c * bw:(c + 1) * bw] = proj(OFF_B + c * bw, bw)


def _inproj(x, m3, row_fn, norm_g, w_in_bf, q_gain, k_gain, rope_tabs, seq_len, emit_kv, tm):
    m = x.shape[0]
    rope = rope_tabs is not None
    row = lambda i: (i, 0)
    const = lambda i: (0, 0)
    in_specs = [
        pl.BlockSpec((tm, D_MODEL), row),
        _mod_spec(0, row_fn),
        _mod_spec(1, row_fn),
        pl.BlockSpec((1, D_MODEL), const),
        pl.BlockSpec((D_MODEL, PROJ_WIDTH), const, pipeline_mode=pl.Buffered(1)),
        pl.BlockSpec((1, HEAD_DIM), const),
        pl.BlockSpec((1, HEAD_DIM), const),
    ]
    args = [x, m3, m3, norm_g, w_in_bf, q_gain, k_gain]
    if rope:
        tiles_per_seq = seq_len // tm
        tab_spec = pl.BlockSpec((tm, HEAD_DIM), lambda i: (i % tiles_per_seq, 0))
        in_specs += [tab_spec, tab_spec]
        args += list(rope_tabs)
    out_specs = [
        pl.BlockSpec((tm, Q_WIDTH), row),
        pl.BlockSpec((tm, KV_WIDTH), row),
        pl.BlockSpec((tm, KV_WIDTH), row),
    ]
    out_shape = [
        jax.ShapeDtypeStruct((m, Q_WIDTH), BF16),
        jax.ShapeDtypeStruct((m, KV_WIDTH), BF16),
        jax.ShapeDtypeStruct((m, KV_WIDTH), BF16),
    ]
    if emit_kv:
        out_specs += [pl.BlockSpec((tm, KV_WIDTH), row)] * 2
        out_shape += [jax.ShapeDtypeStruct((m, KV_WIDTH), F32)] * 2
    out_specs += [pl.BlockSpec((tm, CONV_WIDTH), row)] * 2
    out_shape += [jax.ShapeDtypeStruct((m, CONV_WIDTH), F32)] * 2
    return pl.pallas_call(
        functools.partial(_inproj_kernel, rope=rope, emit_kv=emit_kv),
        grid=(m // tm,),
        in_specs=in_specs,
        out_specs=out_specs,
        out_shape=out_shape,
        scratch_shapes=[pltpu.VMEM((tm, D_MODEL), BF16)],
        compiler_params=_params(("parallel",), 56),
        name="inproj_lat" if rope else "inproj_ctx",
    )(*args)


def _attn_kernel(*refs, tq, tk, n_chunks, has_cache):
    if has_cache:
        q_ref, k_ref, v_ref, ck_ref, cv_ref, o_ref = refs
    else:
        q_ref, k_ref, v_ref, o_ref = refs
    rows = GROUPS * tq
    qs = jnp.concatenate([q_ref[:, g * HEAD_DIM:(g + 1) * HEAD_DIM] for g in range(GROUPS)], axis=0)

    def step(kc, vc, carry):
        m_prev, l_prev, acc = carry
        s = lax.dot_general(qs, kc, (((1,), (1,)), ((), ())), preferred_element_type=F32)
        m_new = jnp.maximum(m_prev, jnp.max(s, axis=-1, keepdims=True))
        alpha = jnp.exp(m_prev - m_new)
        p = jnp.exp(s - m_new)
        l_new = alpha * l_prev + jnp.sum(p, axis=-1, keepdims=True)
        acc = alpha * acc + jnp.dot(p.astype(BF16), vc, preferred_element_type=F32)
        return m_new, l_new, acc

    carry = (jnp.full((rows, 1), -jnp.inf, F32), jnp.zeros((rows, 1), F32), jnp.zeros((rows, HEAD_DIM), F32))
    if has_cache:
        carry = step(ck_ref[...], cv_ref[...], carry)

    def body(c, carry):
        off = pl.multiple_of(c * tk, tk)
        return step(k_ref[pl.ds(off, tk), :], v_ref[pl.ds(off, tk), :], carry)

    _, l_fin, acc = lax.fori_loop(0, n_chunks, body, carry)
    out = acc / l_fin
    for g in range(GROUPS):
        o_ref[:, g * HEAD_DIM:(g + 1) * HEAD_DIM] = out[g * tq:(g + 1) * tq, :].astype(BF16)


def _attention(q, k, v, cache, n_batch, seq_len, tq, tk):
    m = q.shape[0]
    nq = seq_len // tq
    has_cache = cache is not None
    in_specs = [
        pl.BlockSpec((tq, GROUPS * HEAD_DIM), lambda b, h, i: (b * nq + i, h)),
        pl.BlockSpec((seq_len, HEAD_DIM), lambda b, h, i: (b, h)),
        pl.BlockSpec((seq_len, HEAD_DIM), lambda b, h, i: (b, h)),
    ]
    args = [q, k, v]
    if has_cache:
        past = cache[0].shape[1]
        cspec = pl.BlockSpec((None, past, HEAD_DIM), lambda b, h, i: (b, 0, h))
        in_specs += [cspec, cspec]
        args += list(cache)
    return pl.pallas_call(
        functools.partial(_attn_kernel, tq=tq, tk=tk, n_chunks=seq_len // tk, has_cache=has_cache),
        grid=(n_batch, N_KV_HEADS, nq),
        in_specs=in_specs,
        out_specs=pl.BlockSpec((tq, GROUPS * HEAD_DIM), lambda b, h, i: (b * nq + i, h)),
        out_shape=jax.ShapeDtypeStruct((m, Q_WIDTH), BF16),
        compiler_params=_params(("parallel", "parallel", "arbitrary"), 48),
        name="attn_lat" if has_cache else "attn_ctx",
    )(*args)


def _merge_kernel(x_ref, sh_ref, sc_ref, gate_ref, g_ref, attn_ref, cu_ref, cup_ref, cun_ref, bg_ref, cw_ref,
                  wa_ref, ws_ref, wao_ref, wco_ref, wo_ref, o_ref, h_ref, conv_ref, *, tm, seq_len):
    nc = pl.program_id(1)

    @pl.when(nc == 0)
    def _():
        h_ref[...] = _modulated_norm(x_ref[...], g_ref[...], sc_ref[...], sh_ref[...]).astype(BF16)
        cu = cu_ref[...]
        r = lax.broadcasted_iota(jnp.int32, cu.shape, 0)
        p = (pl.program_id(0) * tm + r) % seq_len
        up = pltpu.roll(cu, 1, axis=0)
        up = jnp.where(r == 0, cup_ref[7:8, :], up)
        up = jnp.where(p == 0, 0.0, up)
        dn = pltpu.roll(cu, tm - 1, axis=0)
        dn = jnp.where(r == tm - 1, cun_ref[0:1, :], dn)
        dn = jnp.where(p == seq_len - 1, 0.0, dn)
        conv = up * cw_ref[0:1, :] + cu * cw_ref[1:2, :] + dn * cw_ref[2:3, :]
        conv_ref[...] = (bg_ref[...] * conv).astype(BF16)

    h = h_ref[...]
    ga = jax.nn.sigmoid(jnp.dot(h, wa_ref[...], preferred_element_type=F32))
    gs = jax.nn.sigmoid(jnp.dot(h, ws_ref[...], preferred_element_type=F32))
    ya = jnp.dot(attn_ref[...], wao_ref[...], preferred_element_type=F32)
    yc = jnp.dot(conv_ref[...], wco_ref[...], preferred_element_type=F32)
    merged = (ga * ya + gs * yc).astype(BF16)
    part = jnp.dot(merged, wo_ref[...], preferred_element_type=F32)

    @pl.when(nc == 0)
    def _():
        o_ref[...] = part

    @pl.when(nc > 0)
    def _():
        o_ref[...] += part

    @pl.when(nc == pl.num_programs(1) - 1)
    def _():
        o_ref[...] = x_ref[...] + gate_ref[...] * o_ref[...]


def _merge(x, m3, row_fn, norm_g, attn, cu, bg, conv_w, w_in_bf, wao, wco, wo, seq_len, tm):
    m = x.shape[0]
    tn = 512
    row = lambda i, j: (i, 0)
    const = lambda i, j: (0, 0)
    halo = 8
    blocks_per_tile = tm // halo
    last_halo = m // halo - 1
    in_specs = [
        pl.BlockSpec((tm, D_MODEL), row),
        _mod_spec(0, row_fn),
        _mod_spec(1, row_fn),
        _mod_spec(2, row_fn),
        pl.BlockSpec((1, D_MODEL), const),
        pl.BlockSpec((tm, Q_WIDTH), row),
        pl.BlockSpec((tm, CONV_WIDTH), row),
        pl.BlockSpec((halo, CONV_WIDTH), lambda i, j: (jnp.maximum(i * blocks_per_tile - 1, 0), 0)),
        pl.BlockSpec((halo, CONV_WIDTH), lambda i, j: (jnp.minimum((i + 1) * blocks_per_tile, last_halo), 0)),
        pl.BlockSpec((tm, CONV_WIDTH), row),
        pl.BlockSpec((3, CONV_WIDTH), const),
        pl.BlockSpec((D_MODEL, tn), lambda i, j: (0, OFF_A // tn + j)),
        pl.BlockSpec((D_MODEL, tn), lambda i, j: (0, OFF_S // tn + j)),
        pl.BlockSpec((Q_WIDTH, tn), lambda i, j: (0, j)),
        pl.BlockSpec((CONV_WIDTH, tn), lambda i, j: (0, j)),
        pl.BlockSpec((tn, D_MODEL), lambda i, j: (j, 0)),
    ]
    return pl.pallas_call(
        functools.partial(_merge_kernel, tm=tm, seq_len=seq_len),
        grid=(m // tm, D_MODEL // tn),
        in_specs=in_specs,
        out_specs=pl.BlockSpec((tm, D_MODEL), row),
        out_shape=jax.ShapeDtypeStruct((m, D_MODEL), F32),
        scratch_shapes=[pltpu.VMEM((tm, D_MODEL), BF16), pltpu.VMEM((tm, CONV_WIDTH), BF16)],
        compiler_params=_params(("parallel", "arbitrary"), 56),
        name="merge",
    )(x, m3, m3, m3, norm_g, attn, cu, cu, cu, bg, conv_w, w_in_bf, w_in_bf, wao, wco, wo)


def _mlp_kernel(x_ref, sh_ref, sc_ref, gate_ref, g_ref, fg_ref, w1_ref, w2_ref, o_ref, h_ref):
    fc = pl.program_id(1)

    @pl.when(fc == 0)
    def _():
        h_ref[...] = _modulated_norm(x_ref[...], g_ref[...], sc_ref[...], sh_ref[...]).astype(BF16)

    f = jnp.maximum(jnp.dot(h_ref[...], w1_ref[...], preferred_element_type=F32), 0.0)
    part = jnp.dot((f * f).astype(BF16), w2_ref[...], preferred_element_type=F32)

    @pl.when(fc == 0)
    def _():
        o_ref[...] = part

    @pl.when(fc > 0)
    def _():
        o_ref[...] += part

    @pl.when(fc == pl.num_programs(1) - 1)
    def _():
        y = x_ref[...] + gate_ref[...] * o_ref[...]
        ms = jnp.mean(y * y, axis=-1, keepdims=True)
        o_ref[...] = y * lax.rsqrt(ms + EPS) * fg_ref[...]


def _mlp(x, m3, row_fn, norm_g, final_g, w1, w2, tm, tf):
    m = x.shape[0]
    row = lambda i, j: (i, 0)
    const = lambda i, j: (0, 0)
    in_specs = [
        pl.BlockSpec((tm, D_MODEL), row),
        _mod_spec(3, row_fn),
        _mod_spec(4, row_fn),
        _mod_spec(5, row_fn),
        pl.BlockSpec((1, D_MODEL), const),
        pl.BlockSpec((1, D_MODEL), const),
        pl.BlockSpec((D_MODEL, tf), lambda i, j: (0, j)),
        pl.BlockSpec((tf, D_MODEL), lambda i, j: (j, 0)),
    ]
    return pl.pallas_call(
        _mlp_kernel,
        grid=(m // tm, D_FF // tf),
        in_specs=in_specs,
        out_specs=pl.BlockSpec((tm, D_MODEL), row),
        out_shape=jax.ShapeDtypeStruct((m, D_MODEL), F32),
        scratch_shapes=[pltpu.VMEM((tm, D_MODEL), BF16)],
        compiler_params=_params(("parallel", "arbitrary"), 56),
        name="mlp",
    )(x, m3, m3, m3, norm_g, final_g, w1, w2)


def _rope_tables(n_tokens):
    rows = n_tokens // GRID_W
    row = jnp.broadcast_to(jnp.arange(rows)[:, None], (rows, GRID_W)).reshape(-1).astype(F32)
    col = jnp.broadcast_to(jnp.arange(GRID_W)[None, :], (rows, GRID_W)).reshape(-1).astype(F32)
    half = HEAD_DIM // 2
    inv = ROPE_THETA ** (-jnp.arange(0, half, 2, dtype=F32) / half)
    ang = jnp.concatenate([row[:, None] * inv, col[:, None] * inv], axis=-1)
    cos2 = jnp.repeat(jnp.cos(ang), 2, axis=-1)
    sign = jnp.tile(jnp.array([-1.0, 1.0], F32), half)
    sin2 = jnp.repeat(jnp.sin(ang), 2, axis=-1) * sign
    return cos2, sin2


def _stream(x, m3, row_fn, seq_len, rope_tabs, cache, emit_kv, lw, tiles):
    n_batch = x.shape[0]
    xf = x.reshape(n_batch * seq_len, D_MODEL)
    proj = _inproj(xf, m3, functools.partial(row_fn, tm=tiles["inproj"]), lw["norm1_g"], lw["w_in"], lw["q_gain"],
                   lw["k_gain"], rope_tabs, seq_len, emit_kv, tiles["inproj"])
    if emit_kv:
        q, k, v, kf, vf, cu, bg = proj
    else:
        q, k, v, cu, bg = proj
        kf = vf = None
    attn = _attention(q, k, v, cache, n_batch, seq_len, tiles["tq"], tiles["tk"])
    x1 = _merge(xf, m3, functools.partial(row_fn, tm=tiles["merge"]), lw["norm1_g"], attn, cu, bg, lw["conv_w"],
                lw["w_in"], lw["w_attn_out"], lw["w_conv_out"], lw["w_o"], seq_len, tiles["merge"])
    y = _mlp(x1, m3, functools.partial(row_fn, tm=tiles["mlp"]), lw["norm2_g"], lw["final_g"], lw["w_ff1"],
             lw["w_ff2"], tiles["mlp"], tiles["tf"])
    return y.reshape(x.shape), kf, vf


def _ctx_row(i, tm):
    return 0


def _lat_row(i, tm, seq_len):
    return 1 + (i * tm) // seq_len


def kernel(x_prompt, x_sample, cache_k, cache_v, c, c_ctx, w_mod, b_mod, norm1_g, norm2_g, w_in, q_gain, k_gain,
           conv_w, w_attn_out, w_conv_out, w_o, w_ff1, w_ff2, final_g):
    depth = w_mod.shape[0]
    batch, seq, _ = x_prompt.shape
    dec_batch, dec_seq, _ = x_sample.shape
    past = cache_k.shape[2]
    assert depth == 1 and 1 + dec_batch <= COND_ROWS

    cond = jnp.zeros((COND_ROWS, D_MODEL), F32).at[0].set(c_ctx).at[1:1 + dec_batch].set(c)
    rope_tabs = _rope_tables(dec_seq)

    xp, xs = x_prompt, x_sample
    new_ks, new_vs = [], []
    for l in range(depth):
        lw = {
            "norm1_g": norm1_g[l].reshape(1, D_MODEL),
            "norm2_g": norm2_g[l].reshape(1, D_MODEL),
            "final_g": final_g.reshape(1, D_MODEL),
            "q_gain": q_gain[l].reshape(1, HEAD_DIM),
            "k_gain": k_gain[l].reshape(1, HEAD_DIM),
            "conv_w": conv_w[l],
            "w_in": w_in[l].astype(BF16),
            "w_attn_out": w_attn_out[l].astype(BF16),
            "w_conv_out": w_conv_out[l].astype(BF16),
            "w_o": w_o[l].astype(BF16),
            "w_ff1": w_ff1[l].astype(BF16),
            "w_ff2": w_ff2[l].astype(BF16),
        }
        m3 = _modulation(cond, w_mod[l], b_mod[l]).reshape(COND_ROWS, 1, N_MOD * D_MODEL)
        cache = (cache_k[:, l].reshape(dec_batch, past, KV_WIDTH).astype(BF16),
                 cache_v[:, l].reshape(dec_batch, past, KV_WIDTH).astype(BF16))
        ctx_tiles = {"inproj": 512, "tq": seq, "tk": seq, "merge": 512, "mlp": 512, "tf": 1024}
        lat_tiles = {"inproj": 512, "tq": 256, "tk": 512, "merge": 512, "mlp": 512, "tf": 1024}
        xp, kf, vf = _stream(xp, m3, _ctx_row, seq, None, None, True, lw, ctx_tiles)
        xs, _, _ = _stream(xs, m3, functools.partial(_lat_row, seq_len=dec_seq), dec_seq, rope_tabs, cache, False,
                           lw, lat_tiles)
        new_ks.append(kf.reshape(batch, seq, N_KV_HEADS, HEAD_DIM))
        new_vs.append(vf.reshape(batch, seq, N_KV_HEADS, HEAD_DIM))
    return xp, xs, jnp.stack(new_ks, axis=1), jnp.stack(new_vs, axis=1)
```

```python
import functools

import jax
import jax.numpy as jnp
from jax import lax
from jax.experimental import pallas as pl
from jax.experimental.pallas import tpu as pltpu

D_MODEL = 2048
GRID_W = 64
HEAD_DIM = 128
N_HEADS = 8
N_KV_HEADS = 2
GROUPS = N_HEADS // N_KV_HEADS
Q_WIDTH = N_HEADS * HEAD_DIM
KV_WIDTH = N_KV_HEADS * HEAD_DIM
CONV_WIDTH = D_MODEL // 2
D_FF = 4 * D_MODEL
ROPE_THETA = 10000.0
EPS = 1e-6
N_MOD = 6
COND_ROWS = 16
HALO = 8

OFF_Q = 0
OFF_K = Q_WIDTH
OFF_V = OFF_K + KV_WIDTH
OFF_U = OFF_V + KV_WIDTH
OFF_C = OFF_U + CONV_WIDTH
OFF_B = OFF_C + CONV_WIDTH
OFF_A = OFF_B + CONV_WIDTH
OFF_S = OFF_A + D_MODEL
IN_WIDTH = OFF_S + D_MODEL
PROJ_WIDTH = OFF_A

MIB = 1024 * 1024
F32 = jnp.float32
BF16 = jnp.bfloat16


def _params(sem, vmem_mib):
    return pltpu.CompilerParams(dimension_semantics=sem, vmem_limit_bytes=vmem_mib * MIB)


def _mod_spec(chunk, row_fn):
    return pl.BlockSpec((None, 1, D_MODEL), lambda i, *_: (row_fn(i), 0, chunk))


def _modulated_norm(x, g, sc, sh):
    ms = jnp.mean(x * x, axis=-1, keepdims=True)
    y = x * lax.rsqrt(ms + EPS) * g
    return y * (1.0 + sc) + sh


def _modulated_norm_rows(h_ref, x_ref, g_ref, sc_ref, sh_ref, tm):
    rows = 16
    gs = g_ref[...] * (1.0 + sc_ref[...])
    sh = sh_ref[...]
    for r0 in range(0, tm, rows):
        x = x_ref[r0:r0 + rows, :]
        ms = jnp.mean(x * x, axis=-1, keepdims=True)
        h_ref[r0:r0 + rows, :] = (x * lax.rsqrt(ms + EPS) * gs + sh).astype(BF16)


def _head_norm(z, gain):
    ms = jnp.mean(z * z, axis=-1, keepdims=True)
    return z * lax.rsqrt(ms + EPS) * gain


def _rope(x, cos2, sin2):
    lane = lax.broadcasted_iota(jnp.int32, x.shape, 1)
    nxt = pltpu.roll(x, HEAD_DIM - 1, axis=1)
    prv = pltpu.roll(x, 1, axis=1)
    swapped = jnp.where(lane % 2 == 0, nxt, prv)
    return x * cos2 + swapped * sin2


def _mod_kernel(cond_ref, w_ref, b_ref, o_ref):
    cond = cond_ref[...]
    a = (cond * jax.nn.sigmoid(cond)).astype(BF16)
    o_ref[...] = jnp.dot(a, w_ref[...].astype(BF16), preferred_element_type=F32) + b_ref[...]


def _modulation(cond, w_mod, b_mod):
    tn = 1024
    n = w_mod.shape[1]
    return pl.pallas_call(
        _mod_kernel,
        grid=(n // tn,),
        in_specs=[
            pl.BlockSpec((COND_ROWS, D_MODEL), lambda j: (0, 0)),
            pl.BlockSpec((D_MODEL, tn), lambda j: (0, j)),
            pl.BlockSpec((1, tn), lambda j: (0, j)),
        ],
        out_specs=pl.BlockSpec((COND_ROWS, tn), lambda j: (0, j)),
        out_shape=jax.ShapeDtypeStruct((COND_ROWS, n), F32),
        compiler_params=_params(("arbitrary",), 40),
        name="modulation",
    )(cond, w_mod, b_mod.reshape(1, n))


def _inproj_kernel(*refs, rope, emit_kv):
    x_ref, sh_ref, sc_ref, g_ref, w_ref, qg_ref, kg_ref = refs[:7]
    pos = 7
    if rope:
        cos_ref, sin_ref = refs[pos:pos + 2]
        pos += 2
    q_ref, k_ref, vt_ref = refs[pos:pos + 3]
    pos += 3
    if emit_kv:
        kf_ref, vf_ref = refs[pos:pos + 2]
        pos += 2
    cu_ref, bg_ref, h_ref = refs[pos:pos + 3]

    _modulated_norm_rows(h_ref, x_ref, g_ref, sc_ref, sh_ref, x_ref.shape[0])

    def proj(off, width):
        return jnp.dot(h_ref[...], w_ref[:, off:off + width], preferred_element_type=F32)

    if rope:
        cos2 = cos_ref[...]
        sin2 = sin_ref[...]
    scale = HEAD_DIM ** -0.5

    for hd in range(N_HEADS // 4):
        z = proj(OFF_Q + hd * 4 * HEAD_DIM, 4 * HEAD_DIM)
        for hh in range(4):
            qn = _head_norm(z[:, hh * HEAD_DIM:(hh + 1) * HEAD_DIM], qg_ref[...])
            if rope:
                qn = _rope(qn, cos2, sin2)
            col = (hd * 4 + hh) * HEAD_DIM
            q_ref[:, col:col + HEAD_DIM] = (qn * scale).astype(BF16)

    z = proj(OFF_K, 2 * KV_WIDTH)
    for hh in range(N_KV_HEADS):
        col = hh * HEAD_DIM
        kn = _head_norm(z[:, col:col + HEAD_DIM], kg_ref[...])
        if emit_kv:
            kf_ref[:, col:col + HEAD_DIM] = kn
        if rope:
            kn = _rope(kn, cos2, sin2)
        k_ref[:, col:col + HEAD_DIM] = kn.astype(BF16)
    vv = z[:, KV_WIDTH:]
    if emit_kv:
        vf_ref[...] = vv
    vt_ref[...] = vv.T.astype(BF16)

    cw = 256
    for c in range(CONV_WIDTH // cw):
        u = proj(OFF_U + c * cw, cw)
        gc = proj(OFF_C + c * cw, cw)
        cu_ref[:, c * cw:(c + 1) * cw] = gc * u

    bw = 512
    for c in range(CONV_WIDTH // bw):
        bg_ref[:, c * bw:(c + 1) * bw] = proj(OFF_B + c * bw, bw)


def _inproj(x, m3, row_fn, norm_g, w_in_bf, q_gain, k_gain, rope_tabs, seq_len, emit_kv, tm):
    m = x.shape[0]
    rope = rope_tabs is not None
    row = lambda i: (i, 0)
    const = lambda i: (0, 0)
    in_specs = [
        pl.BlockSpec((tm, D_MODEL), row),
        _mod_spec(0, row_fn),
        _mod_spec(1, row_fn),
        pl.BlockSpec((1, D_MODEL), const),
        pl.BlockSpec((D_MODEL, PROJ_WIDTH), const, pipeline_mode=pl.Buffered(1)),
        pl.BlockSpec((1, HEAD_DIM), const),
        pl.BlockSpec((1, HEAD_DIM), const),
    ]
    args = [x, m3, m3, norm_g, w_in_bf, q_gain, k_gain]
    if rope:
        tiles_per_seq = seq_len // tm
        tab_spec = pl.BlockSpec((tm, HEAD_DIM), lambda i: (i % tiles_per_seq, 0))
        in_specs += [tab_spec, tab_spec]
        args += list(rope_tabs)
    out_specs = [
        pl.BlockSpec((tm, Q_WIDTH), row),
        pl.BlockSpec((tm, KV_WIDTH), row),
        pl.BlockSpec((KV_WIDTH, tm), lambda i: (0, i)),
    ]
    out_shape = [
        jax.ShapeDtypeStruct((m, Q_WIDTH), BF16),
        jax.ShapeDtypeStruct((m, KV_WIDTH), BF16),
        jax.ShapeDtypeStruct((KV_WIDTH, m), BF16),
    ]
    if emit_kv:
        out_specs += [pl.BlockSpec((tm, KV_WIDTH), row)] * 2
        out_shape += [jax.ShapeDtypeStruct((m, KV_WIDTH), F32)] * 2
    out_specs += [pl.BlockSpec((tm, CONV_WIDTH), row)] * 2
    out_shape += [jax.ShapeDtypeStruct((m, CONV_WIDTH), F32)] * 2
    return pl.pallas_call(
        functools.partial(_inproj_kernel, rope=rope, emit_kv=emit_kv),
        grid=(m // tm,),
        in_specs=in_specs,
        out_specs=out_specs,
        out_shape=out_shape,
        scratch_shapes=[pltpu.VMEM((tm, D_MODEL), BF16)],
        compiler_params=_params(("parallel",), 56),
        name="inproj_lat" if rope else "inproj_ctx",
    )(*args)


def _attn_kernel(*refs, tq, has_cache):
    if has_cache:
        q_ref, k_ref, vt_ref, ck_ref, cv_ref, o_ref, s_ref = refs
        past = ck_ref.shape[0]
    else:
        q_ref, k_ref, vt_ref, o_ref, s_ref = refs
        past = 0
    nt = (((1,), (1,)), ((), ()))
    qs = jnp.concatenate([q_ref[:, g * HEAD_DIM:(g + 1) * HEAD_DIM] for g in range(GROUPS)], axis=0)
    if has_cache:
        s_ref[:past, :] = lax.dot_general(ck_ref[...].astype(BF16), qs, nt, preferred_element_type=F32)
    s_ref[past:, :] = lax.dot_general(k_ref[...], qs, nt, preferred_element_type=F32)
    m = jnp.max(s_ref[...], axis=0, keepdims=True)
    p = jnp.exp(s_ref[...] - m)
    l_fin = jnp.sum(p, axis=0, keepdims=True)
    pb = p.astype(BF16)
    acc = jnp.dot(vt_ref[...], pb[past:, :], preferred_element_type=F32)
    if has_cache:
        acc = acc + jnp.dot(cv_ref[...].T.astype(BF16), pb[:past, :], preferred_element_type=F32)
    out_t = acc / l_fin
    for g in range(GROUPS):
        o_ref[:, g * HEAD_DIM:(g + 1) * HEAD_DIM] = out_t[:, g * tq:(g + 1) * tq].T.astype(BF16)


def _attention(q, k, vt, cache, n_batch, seq_len, tq):
    m = q.shape[0]
    nq = seq_len // tq
    has_cache = cache is not None
    in_specs = [
        pl.BlockSpec((tq, GROUPS * HEAD_DIM), lambda b, h, i: (b * nq + i, h)),
        pl.BlockSpec((seq_len, HEAD_DIM), lambda b, h, i: (b, h)),
        pl.BlockSpec((HEAD_DIM, seq_len), lambda b, h, i: (h, b)),
    ]
    args = [q, k, vt]
    past = 0
    if has_cache:
        past = cache[0].shape[1]
        in_specs += [pl.BlockSpec((None, past, HEAD_DIM), lambda b, h, i: (b, 0, h))] * 2
        args += list(cache)
    return pl.pallas_call(
        functools.partial(_attn_kernel, tq=tq, has_cache=has_cache),
        scratch_shapes=[pltpu.VMEM((past + seq_len, GROUPS * tq), F32)],
        grid=(n_batch, N_KV_HEADS, nq),
        in_specs=in_specs,
        out_specs=pl.BlockSpec((tq, GROUPS * HEAD_DIM), lambda b, h, i: (b * nq + i, h)),
        out_shape=jax.ShapeDtypeStruct((m, Q_WIDTH), BF16),
        compiler_params=_params(("parallel", "parallel", "arbitrary"), 48),
        name="attn_lat" if has_cache else "attn_ctx",
    )(*args)


def _merge_kernel(x_ref, sh_ref, sc_ref, gate_ref, g_ref, attn_ref, cu_ref, cup_ref, cun_ref, bg_ref, cw_ref,
                  wa_ref, ws_ref, wao_ref, wco_ref, wo_ref, o_ref, h_ref, conv_ref, *, tm, seq_len):
    nc = pl.program_id(1)
    last = pl.num_programs(1) - 1

    def conv_rows():
        rows = 32
        w0, w1, w2 = cw_ref[0:1, :], cw_ref[1:2, :], cw_ref[2:3, :]
        zeros = jnp.zeros((HALO, CONV_WIDTH), F32)
        if seq_len > tm:
            tiles_per_seq = seq_len // tm
            tile_in_seq = jnp.full((HALO, CONV_WIDTH), pl.program_id(0) % tiles_per_seq, jnp.int32)
            first_halo = jnp.where(tile_in_seq == 0, 0.0, cup_ref[...])
            last_halo = jnp.where(tile_in_seq == tiles_per_seq - 1, 0.0, cun_ref[...])
        else:
            first_halo = last_halo = zeros
        for r0 in range(0, tm, rows):
            end = r0 + rows
            cur = cu_ref[r0:end, :]
            if r0 == 0:
                prev = first_halo
            elif r0 % seq_len == 0:
                prev = zeros
            else:
                prev = cu_ref[r0 - HALO:r0, :]
            if end == tm:
                nxt = last_halo
            elif end % seq_len == 0:
                nxt = zeros
            else:
                nxt = cu_ref[end:end + HALO, :]
            ext = jnp.concatenate([prev, cur, nxt], axis=0)
            up = pltpu.roll(ext, 1, axis=0)[HALO:HALO + rows]
            dn = pltpu.roll(ext, rows + 2 * HALO - 1, axis=0)[HALO:HALO + rows]
            conv = up * w0 + cur * w1 + dn * w2
            conv_ref[r0:end, :] = (bg_ref[r0:end, :] * conv).astype(BF16)

    def compute(first):
        ya = jnp.dot(attn_ref[...], wao_ref[...], preferred_element_type=F32)
        if first:
            _modulated_norm_rows(h_ref, x_ref, g_ref, sc_ref, sh_ref, tm)
            conv_rows()
        h = h_ref[...]
        ga = jax.nn.sigmoid(jnp.dot(h, wa_ref[...], preferred_element_type=F32))
        gs = jax.nn.sigmoid(jnp.dot(h, ws_ref[...], preferred_element_type=F32))
        yc = jnp.dot(conv_ref[...], wco_ref[...], preferred_element_type=F32)
        merged = (ga * ya + gs * yc).astype(BF16)
        return jnp.dot(merged, wo_ref[...], preferred_element_type=F32)

    @pl.when(nc == 0)
    def _():
        o_ref[...] = compute(True)

    @pl.when((nc > 0) & (nc < last))
    def _():
        o_ref[...] += compute(False)

    @pl.when(nc == last)
    def _():
        o_ref[...] = x_ref[...] + gate_ref[...] * (o_ref[...] + compute(False))


def _merge(x, m3, row_fn, norm_g, attn, cu, bg, conv_w, w_in_bf, wao, wco, wo, seq_len, tm):
    m = x.shape[0]
    tn = 512
    row = lambda i, j: (i, 0)
    const = lambda i, j: (0, 0)
    assert seq_len % tm == 0 or tm % seq_len == 0
    blocks_per_tile = tm // HALO
    last_halo = m // HALO - 1
    in_specs = [
        pl.BlockSpec((tm, D_MODEL), row),
        _mod_spec(0, row_fn),
        _mod_spec(1, row_fn),
        _mod_spec(2, row_fn),
        pl.BlockSpec((1, D_MODEL), const),
        pl.BlockSpec((tm, Q_WIDTH), row),
        pl.BlockSpec((tm, CONV_WIDTH), row),
        pl.BlockSpec((HALO, CONV_WIDTH), lambda i, j: (jnp.maximum(i * blocks_per_tile - 1, 0), 0)),
        pl.BlockSpec((HALO, CONV_WIDTH), lambda i, j: (jnp.minimum((i + 1) * blocks_per_tile, last_halo), 0)),
        pl.BlockSpec((tm, CONV_WIDTH), row),
        pl.BlockSpec((3, CONV_WIDTH), const),
        pl.BlockSpec((D_MODEL, tn), lambda i, j: (0, OFF_A // tn + j)),
        pl.BlockSpec((D_MODEL, tn), lambda i, j: (0, OFF_S // tn + j)),
        pl.BlockSpec((Q_WIDTH, tn), lambda i, j: (0, j)),
        pl.BlockSpec((CONV_WIDTH, tn), lambda i, j: (0, j)),
        pl.BlockSpec((tn, D_MODEL), lambda i, j: (j, 0)),
    ]
    return pl.pallas_call(
        functools.partial(_merge_kernel, tm=tm, seq_len=seq_len),
        grid=(m // tm, D_MODEL // tn),
        in_specs=in_specs,
        out_specs=pl.BlockSpec((tm, D_MODEL), row),
        out_shape=jax.ShapeDtypeStruct((m, D_MODEL), F32),
        scratch_shapes=[pltpu.VMEM((tm, D_MODEL), BF16), pltpu.VMEM((tm, CONV_WIDTH), BF16)],
        compiler_params=_params(("parallel", "arbitrary"), 56),
        name="merge",
    )(x, m3, m3, m3, norm_g, attn, cu, cu, cu, bg, conv_w, w_in_bf, w_in_bf, wao, wco, wo)


def _mlp_kernel(x_ref, sh_ref, sc_ref, gate_ref, g_ref, fg_ref, w1_ref, w2_ref, o_ref, h_ref):
    fc = pl.program_id(1)
    last = pl.num_programs(1) - 1
    tm = x_ref.shape[0]

    def part():
        f = jnp.maximum(jnp.dot(h_ref[...], w1_ref[...], preferred_element_type=F32), 0.0)
        return jnp.dot((f * f).astype(BF16), w2_ref[...], preferred_element_type=F32)

    @pl.when(fc == 0)
    def _():
        _modulated_norm_rows(h_ref, x_ref, g_ref, sc_ref, sh_ref, tm)
        o_ref[...] = part()

    @pl.when((fc > 0) & (fc < last))
    def _():
        o_ref[...] += part()

    @pl.when(fc == last)
    def _():
        o_ref[...] += part()
        gate = gate_ref[...]
        fg = fg_ref[...]
        rows = 16
        for r0 in range(0, tm, rows):
            y = x_ref[r0:r0 + rows, :] + gate * o_ref[r0:r0 + rows, :]
            ms = jnp.mean(y * y, axis=-1, keepdims=True)
            o_ref[r0:r0 + rows, :] = y * lax.rsqrt(ms + EPS) * fg


def _mlp(x, m3, row_fn, norm_g, final_g, w1, w2, tm, tf):
    m = x.shape[0]
    row = lambda i, j: (i, 0)
    const = lambda i, j: (0, 0)
    in_specs = [
        pl.BlockSpec((tm, D_MODEL), row),
        _mod_spec(3, row_fn),
        _mod_spec(4, row_fn),
        _mod_spec(5, row_fn),
        pl.BlockSpec((1, D_MODEL), const),
        pl.BlockSpec((1, D_MODEL), const),
        pl.BlockSpec((D_MODEL, tf), lambda i, j: (0, j)),
        pl.BlockSpec((tf, D_MODEL), lambda i, j: (j, 0)),
    ]
    return pl.pallas_call(
        _mlp_kernel,
        grid=(m // tm, D_FF // tf),
        in_specs=in_specs,
        out_specs=pl.BlockSpec((tm, D_MODEL), row),
        out_shape=jax.ShapeDtypeStruct((m, D_MODEL), F32),
        scratch_shapes=[pltpu.VMEM((tm, D_MODEL), BF16)],
        compiler_params=_params(("parallel", "arbitrary"), 56),
        name="mlp",
    )(x, m3, m3, m3, norm_g, final_g, w1, w2)


def _rope_tables(n_tokens):
    rows = n_tokens // GRID_W
    row = jnp.broadcast_to(jnp.arange(rows)[:, None], (rows, GRID_W)).reshape(-1).astype(F32)
    col = jnp.broadcast_to(jnp.arange(GRID_W)[None, :], (rows, GRID_W)).reshape(-1).astype(F32)
    half = HEAD_DIM // 2
    inv = ROPE_THETA ** (-jnp.arange(0, half, 2, dtype=F32) / half)
    ang = jnp.concatenate([row[:, None] * inv, col[:, None] * inv], axis=-1)
    cos2 = jnp.repeat(jnp.cos(ang), 2, axis=-1)
    sign = jnp.tile(jnp.array([-1.0, 1.0], F32), half)
    sin2 = jnp.repeat(jnp.sin(ang), 2, axis=-1) * sign
    return cos2, sin2


def _stream(x, m3, row_fn, seq_len, rope_tabs, cache, emit_kv, lw, tiles):
    n_batch = x.shape[0]
    xf = x.reshape(n_batch * seq_len, D_MODEL)
    proj = _inproj(xf, m3, functools.partial(row_fn, tm=tiles["inproj"]), lw["norm1_g"], lw["w_in"], lw["q_gain"],
                   lw["k_gain"], rope_tabs, seq_len, emit_kv, tiles["inproj"])
    if emit_kv:
        q, k, vt, kf, vf, cu, bg = proj
    else:
        q, k, vt, cu, bg = proj
        kf = vf = None
    attn = _attention(q, k, vt, cache, n_batch, seq_len, tiles["tq"])
    x1 = _merge(xf, m3, functools.partial(row_fn, tm=tiles["merge"]), lw["norm1_g"], attn, cu, bg, lw["conv_w"],
                lw["w_in"], lw["w_attn_out"], lw["w_conv_out"], lw["w_o"], seq_len, tiles["merge"])
    y = _mlp(x1, m3, functools.partial(row_fn, tm=tiles["mlp"]), lw["norm2_g"], lw["final_g"], lw["w_ff1"],
             lw["w_ff2"], tiles["mlp"], tiles["tf"])
    return y.reshape(x.shape), kf, vf


def _ctx_row(i, tm):
    return 0


def _lat_row(i, tm, seq_len):
    return 1 + (i * tm) // seq_len


def kernel(x_prompt, x_sample, cache_k, cache_v, c, c_ctx, w_mod, b_mod, norm1_g, norm2_g, w_in, q_gain, k_gain,
           conv_w, w_attn_out, w_conv_out, w_o, w_ff1, w_ff2, final_g):
    depth = w_mod.shape[0]
    batch, seq, _ = x_prompt.shape
    dec_batch, dec_seq, _ = x_sample.shape
    past = cache_k.shape[2]
    assert depth == 1 and 1 + dec_batch <= COND_ROWS

    cond = jnp.zeros((COND_ROWS, D_MODEL), F32).at[0].set(c_ctx).at[1:1 + dec_batch].set(c)
    rope_tabs = _rope_tables(dec_seq)

    xp, xs = x_prompt, x_sample
    new_ks, new_vs = [], []
    for l in range(depth):
        lw = {
            "norm1_g": norm1_g[l].reshape(1, D_MODEL),
            "norm2_g": norm2_g[l].reshape(1, D_MODEL),
            "final_g": final_g.reshape(1, D_MODEL),
            "q_gain": q_gain[l].reshape(1, HEAD_DIM),
            "k_gain": k_gain[l].reshape(1, HEAD_DIM),
            "conv_w": conv_w[l],
            "w_in": w_in[l].astype(BF16),
            "w_attn_out": w_attn_out[l].astype(BF16),
            "w_conv_out": w_conv_out[l].astype(BF16),
            "w_o": w_o[l].astype(BF16),
            "w_ff1": w_ff1[l].astype(BF16),
            "w_ff2": w_ff2[l].astype(BF16),
        }
        m3 = _modulation(cond, w_mod[l], b_mod[l]).reshape(COND_ROWS, 1, N_MOD * D_MODEL)
        cache = (cache_k[:, l].reshape(dec_batch, past, KV_WIDTH), cache_v[:, l].reshape(dec_batch, past, KV_WIDTH))
        ctx_tiles = {"inproj": 512, "tq": seq, "tk": seq, "merge": 512, "mlp": 512, "tf": 1024}
        lat_tiles = {"inproj": 512, "tq": 256, "tk": 512, "merge": 512, "mlp": 512, "tf": 1024}
        xp, kf, vf = _stream(xp, m3, _ctx_row, seq, None, None, True, lw, ctx_tiles)
        xs, _, _ = _stream(xs, m3, functools.partial(_lat_row, seq_len=dec_seq), dec_seq, rope_tabs, cache, False,
                           lw, lat_tiles)
        new_ks.append(kf.reshape(batch, seq, N_KV_HEADS, HEAD_DIM))
        new_vs.append(vf.reshape(batch, seq, N_KV_HEADS, HEAD_DIM))
    return xp, xs, jnp.stack(new_ks, axis=1), jnp.stack(new_vs, axis=1)
```

```python
import functools

import jax
import jax.numpy as jnp
from jax import lax
from jax.experimental import pallas as pl
from jax.experimental.pallas import tpu as pltpu

D_MODEL = 2048
GRID_W = 64
HEAD_DIM = 128
N_HEADS = 8
N_KV_HEADS = 2
GROUPS = N_HEADS // N_KV_HEADS
Q_WIDTH = N_HEADS * HEAD_DIM
KV_WIDTH = N_KV_HEADS * HEAD_DIM
CONV_WIDTH = D_MODEL // 2
D_FF = 4 * D_MODEL
ROPE_THETA = 10000.0
EPS = 1e-6
N_MOD = 6
LOG2_E = 1.4426950408889634
COND_ROWS = 16
HALO = 8

OFF_Q = 0
OFF_K = Q_WIDTH
OFF_V = OFF_K + KV_WIDTH
OFF_U = OFF_V + KV_WIDTH
OFF_C = OFF_U + CONV_WIDTH
OFF_B = OFF_C + CONV_WIDTH
OFF_A = OFF_B + CONV_WIDTH
OFF_S = OFF_A + D_MODEL
IN_WIDTH = OFF_S + D_MODEL
PROJ_WIDTH = OFF_A

MIB = 1024 * 1024
F32 = jnp.float32
BF16 = jnp.bfloat16


def _params(sem, vmem_mib):
    return pltpu.CompilerParams(dimension_semantics=sem, vmem_limit_bytes=vmem_mib * MIB)


def _mod_spec(chunk, row_fn):
    return pl.BlockSpec((None, 1, D_MODEL), lambda i, *_: (row_fn(i), 0, chunk))


def _modulated_norm(x, g, sc, sh):
    ms = jnp.mean(x * x, axis=-1, keepdims=True)
    y = x * lax.rsqrt(ms + EPS) * g
    return y * (1.0 + sc) + sh


def _modulated_norm_rows(h_ref, x_ref, g_ref, sc_ref, sh_ref, tm):
    rows = 16
    gs = g_ref[...] * (1.0 + sc_ref[...])
    sh = sh_ref[...]
    for r0 in range(0, tm, rows):
        x = x_ref[r0:r0 + rows, :]
        ms = jnp.mean(x * x, axis=-1, keepdims=True)
        h_ref[r0:r0 + rows, :] = (x * lax.rsqrt(ms + EPS) * gs + sh).astype(BF16)


def _head_norm(z, gain):
    ms = jnp.mean(z * z, axis=-1, keepdims=True)
    return z * lax.rsqrt(ms + EPS) * gain


def _rope(x, cos2, sin2):
    lane = lax.broadcasted_iota(jnp.int32, x.shape, 1)
    nxt = pltpu.roll(x, HEAD_DIM - 1, axis=1)
    prv = pltpu.roll(x, 1, axis=1)
    swapped = jnp.where(lane % 2 == 0, nxt, prv)
    return x * cos2 + swapped * sin2


def _mod_kernel(cond_ref, w_ref, b_ref, o_ref):
    cond = cond_ref[...]
    a = (cond * jax.nn.sigmoid(cond)).astype(BF16)
    o_ref[...] = jnp.dot(a, w_ref[...].astype(BF16), preferred_element_type=F32) + b_ref[...]


def _modulation(cond, w_mod, b_mod):
    tn = 1024
    n = w_mod.shape[1]
    return pl.pallas_call(
        _mod_kernel,
        grid=(n // tn,),
        in_specs=[
            pl.BlockSpec((COND_ROWS, D_MODEL), lambda j: (0, 0)),
            pl.BlockSpec((D_MODEL, tn), lambda j: (0, j)),
            pl.BlockSpec((1, tn), lambda j: (0, j)),
        ],
        out_specs=pl.BlockSpec((COND_ROWS, tn), lambda j: (0, j)),
        out_shape=jax.ShapeDtypeStruct((COND_ROWS, n), F32),
        compiler_params=_params(("arbitrary",), 40),
        name="modulation",
    )(cond, w_mod, b_mod.reshape(1, n))


def _inproj_kernel(*refs, rope, emit_kv):
    x_ref, sh_ref, sc_ref, g_ref, w_ref, qg_ref, kg_ref = refs[:7]
    pos = 7
    if rope:
        cos_ref, sin_ref = refs[pos:pos + 2]
        pos += 2
    q_ref, k_ref, vt_ref = refs[pos:pos + 3]
    pos += 3
    if emit_kv:
        kf_ref, vf_ref = refs[pos:pos + 2]
        pos += 2
    cu_ref, bg_ref, h_ref = refs[pos:pos + 3]

    _modulated_norm_rows(h_ref, x_ref, g_ref, sc_ref, sh_ref, x_ref.shape[0])

    def proj(off, width):
        return jnp.dot(h_ref[...], w_ref[:, off:off + width], preferred_element_type=F32)

    if rope:
        cos2 = cos_ref[...]
        sin2 = sin_ref[...]
    scale = HEAD_DIM ** -0.5 * LOG2_E

    for hd in range(N_HEADS // 4):
        z = proj(OFF_Q + hd * 4 * HEAD_DIM, 4 * HEAD_DIM)
        for hh in range(4):
            qn = _head_norm(z[:, hh * HEAD_DIM:(hh + 1) * HEAD_DIM], qg_ref[...])
            if rope:
                qn = _rope(qn, cos2, sin2)
            col = (hd * 4 + hh) * HEAD_DIM
            q_ref[:, col:col + HEAD_DIM] = (qn * scale).astype(BF16)

    z = proj(OFF_K, 2 * KV_WIDTH)
    for hh in range(N_KV_HEADS):
        col = hh * HEAD_DIM
        kn = _head_norm(z[:, col:col + HEAD_DIM], kg_ref[...])
        if emit_kv:
            kf_ref[:, col:col + HEAD_DIM] = kn
        if rope:
            kn = _rope(kn, cos2, sin2)
        k_ref[:, col:col + HEAD_DIM] = kn.astype(BF16)
    vv = z[:, KV_WIDTH:]
    if emit_kv:
        vf_ref[...] = vv
    vt_ref[...] = vv.T.astype(BF16)

    cw = 256
    for c in range(CONV_WIDTH // cw):
        u = proj(OFF_U + c * cw, cw)
        gc = proj(OFF_C + c * cw, cw)
        cu_ref[:, c * cw:(c + 1) * cw] = gc * u

    bw = 512
    for c in range(CONV_WIDTH // bw):
        bg_ref[:, c * bw:(c + 1) * bw] = proj(OFF_B + c * bw, bw)


def _inproj(x, m3, row_fn, norm_g, w_in_bf, q_gain, k_gain, rope_tabs, seq_len, emit_kv, tm):
    m = x.shape[0]
    rope = rope_tabs is not None
    row = lambda i: (i, 0)
    const = lambda i: (0, 0)
    in_specs = [
        pl.BlockSpec((tm, D_MODEL), row),
        _mod_spec(0, row_fn),
        _mod_spec(1, row_fn),
        pl.BlockSpec((1, D_MODEL), const),
        pl.BlockSpec((D_MODEL, PROJ_WIDTH), const, pipeline_mode=pl.Buffered(1)),
        pl.BlockSpec((1, HEAD_DIM), const),
        pl.BlockSpec((1, HEAD_DIM), const),
    ]
    args = [x, m3, m3, norm_g, w_in_bf, q_gain, k_gain]
    if rope:
        tiles_per_seq = seq_len // tm
        tab_spec = pl.BlockSpec((tm, HEAD_DIM), lambda i: (i % tiles_per_seq, 0))
        in_specs += [tab_spec, tab_spec]
        args += list(rope_tabs)
    out_specs = [
        pl.BlockSpec((tm, Q_WIDTH), row),
        pl.BlockSpec((tm, KV_WIDTH), row),
        pl.BlockSpec((KV_WIDTH, tm), lambda i: (0, i)),
    ]
    out_shape = [
        jax.ShapeDtypeStruct((m, Q_WIDTH), BF16),
        jax.ShapeDtypeStruct((m, KV_WIDTH), BF16),
        jax.ShapeDtypeStruct((KV_WIDTH, m), BF16),
    ]
    if emit_kv:
        out_specs += [pl.BlockSpec((tm, KV_WIDTH), row)] * 2
        out_shape += [jax.ShapeDtypeStruct((m, KV_WIDTH), F32)] * 2
    out_specs += [pl.BlockSpec((tm, CONV_WIDTH), row)] * 2
    out_shape += [jax.ShapeDtypeStruct((m, CONV_WIDTH), F32)] * 2
    return pl.pallas_call(
        functools.partial(_inproj_kernel, rope=rope, emit_kv=emit_kv),
        grid=(m // tm,),
        in_specs=in_specs,
        out_specs=out_specs,
        out_shape=out_shape,
        scratch_shapes=[pltpu.VMEM((tm, D_MODEL), BF16)],
        compiler_params=_params(("parallel",), 56),
        name="inproj_lat" if rope else "inproj_ctx",
    )(*args)


def _attn_kernel(*refs, tq, n_tiles, has_cache):
    if has_cache:
        q_ref, k_ref, vt_ref, ck_ref, cv_ref, o_ref, s0_ref, s1_ref, m0_ref, m1_ref = refs
        past = ck_ref.shape[0]
    else:
        q_ref, k_ref, vt_ref, o_ref, s0_ref, s1_ref, m0_ref, m1_ref = refs
        past = 0
    t = pl.program_id(0)
    nt = (((1,), (1,)), ((), ()))

    def score_pass(s_ref, m_ref):
        qs = jnp.concatenate([q_ref[:, g * HEAD_DIM:(g + 1) * HEAD_DIM] for g in range(GROUPS)], axis=0)
        if has_cache:
            s_ref[:past, :] = lax.dot_general(ck_ref[...].astype(BF16), qs, nt, preferred_element_type=F32)
        s_ref[past:, :] = lax.dot_general(k_ref[...], qs, nt, preferred_element_type=F32)
        m_ref[...] = jnp.max(s_ref[...], axis=0, keepdims=True)

    def value_pass(s_ref, m_ref):
        p = jnp.exp2(s_ref[...] - m_ref[...])
        l_fin = jnp.sum(p, axis=0, keepdims=True)
        pb = p.astype(BF16)
        acc = jnp.dot(vt_ref[...], pb[past:, :], preferred_element_type=F32)
        if has_cache:
            acc = acc + jnp.dot(cv_ref[...].T.astype(BF16), pb[:past, :], preferred_element_type=F32)
        out_t = acc / l_fin
        for g in range(GROUPS):
            o_ref[:, g * HEAD_DIM:(g + 1) * HEAD_DIM] = out_t[:, g * tq:(g + 1) * tq].T.astype(BF16)

    slots = ((s0_ref, m0_ref), (s1_ref, m1_ref))

    @pl.when(t == 0)
    def _():
        score_pass(*slots[0])

    for parity in range(2):
        @pl.when((t > 0) & (t < n_tiles) & (t % 2 == parity))
        def _():
            score_pass(*slots[parity])
            value_pass(*slots[1 - parity])

    @pl.when(t == n_tiles)
    def _():
        value_pass(*slots[(n_tiles - 1) % 2])


def _attention(q, k, vt, cache, n_batch, seq_len, tq):
    m = q.shape[0]
    nq = seq_len // tq
    n_tiles = n_batch * N_KV_HEADS * nq
    has_cache = cache is not None

    def tile(u):
        return u // (N_KV_HEADS * nq), (u // nq) % N_KV_HEADS, u % nq

    def score_tile(t):
        return tile(jnp.minimum(t, n_tiles - 1))

    def value_tile(t):
        return tile(jnp.maximum(t - 1, 0))

    def q_map(t):
        b, h, i = score_tile(t)
        return b * nq + i, h

    def k_map(t):
        b, h, _ = score_tile(t)
        return b, h

    def ck_map(t):
        b, h, _ = score_tile(t)
        return b, 0, h

    def vt_map(t):
        b, h, _ = value_tile(t)
        return h, b

    def cv_map(t):
        b, h, _ = value_tile(t)
        return b, 0, h

    def o_map(t):
        b, h, i = value_tile(t)
        return b * nq + i, h

    in_specs = [
        pl.BlockSpec((tq, GROUPS * HEAD_DIM), q_map),
        pl.BlockSpec((seq_len, HEAD_DIM), k_map),
        pl.BlockSpec((HEAD_DIM, seq_len), vt_map),
    ]
    args = [q, k, vt]
    past = 0
    if has_cache:
        past = cache[0].shape[1]
        in_specs += [pl.BlockSpec((None, past, HEAD_DIM), ck_map), pl.BlockSpec((None, past, HEAD_DIM), cv_map)]
        args += list(cache)
    cols = GROUPS * tq
    score_buf = pltpu.VMEM((past + seq_len, cols), F32)
    max_buf = pltpu.VMEM((1, cols), F32)
    return pl.pallas_call(
        functools.partial(_attn_kernel, tq=tq, n_tiles=n_tiles, has_cache=has_cache),
        scratch_shapes=[score_buf, score_buf, max_buf, max_buf],
        grid=(n_tiles + 1,),
        in_specs=in_specs,
        out_specs=pl.BlockSpec((tq, GROUPS * HEAD_DIM), o_map),
        out_shape=jax.ShapeDtypeStruct((m, Q_WIDTH), BF16),
        compiler_params=_params(("arbitrary",), 56),
        name="attn_lat" if has_cache else "attn_ctx",
    )(*args)


def _merge_kernel(x_ref, sh_ref, sc_ref, gate_ref, g_ref, attn_ref, cu_ref, cup_ref, cun_ref, bg_ref, cw_ref,
                  wg_ref, wao_ref, wco_ref, wo_ref, o_ref, h_ref, conv_ref, merged_ref, *, tm, seq_len):

    def conv_rows():
        rows = 32
        w0, w1, w2 = cw_ref[0:1, :], cw_ref[1:2, :], cw_ref[2:3, :]
        zeros = jnp.zeros((HALO, CONV_WIDTH), F32)
        if seq_len > tm:
            tiles_per_seq = seq_len // tm
            tile_in_seq = jnp.full((HALO, CONV_WIDTH), pl.program_id(0) % tiles_per_seq, jnp.int32)
            first_halo = jnp.where(tile_in_seq == 0, 0.0, cup_ref[...])
            last_halo = jnp.where(tile_in_seq == tiles_per_seq - 1, 0.0, cun_ref[...])
        else:
            first_halo = last_halo = zeros
        for r0 in range(0, tm, rows):
            end = r0 + rows
            cur = cu_ref[r0:end, :]
            if r0 == 0:
                prev = first_halo
            elif r0 % seq_len == 0:
                prev = zeros
            else:
                prev = cu_ref[r0 - HALO:r0, :]
            if end == tm:
                nxt = last_halo
            elif end % seq_len == 0:
                nxt = zeros
            else:
                nxt = cu_ref[end:end + HALO, :]
            ext = jnp.concatenate([prev, cur, nxt], axis=0)
            up = pltpu.roll(ext, 1, axis=0)[HALO:HALO + rows]
            dn = pltpu.roll(ext, rows + 2 * HALO - 1, axis=0)[HALO:HALO + rows]
            conv = up * w0 + cur * w1 + dn * w2
            conv_ref[r0:end, :] = (bg_ref[r0:end, :] * conv).astype(BF16)

    _modulated_norm_rows(h_ref, x_ref, g_ref, sc_ref, sh_ref, tm)
    conv_rows()
    tn = 512
    for c0 in range(0, D_MODEL, tn):
        ya = jnp.dot(attn_ref[...], wao_ref[:, c0:c0 + tn], preferred_element_type=F32)
        ga = jax.nn.sigmoid(jnp.dot(h_ref[...], wg_ref[:, c0:c0 + tn], preferred_element_type=F32))
        gs = jax.nn.sigmoid(jnp.dot(h_ref[...], wg_ref[:, D_MODEL + c0:D_MODEL + c0 + tn], preferred_element_type=F32))
        yc = jnp.dot(conv_ref[...], wco_ref[:, c0:c0 + tn], preferred_element_type=F32)
        merged_ref[:, c0:c0 + tn] = (ga * ya + gs * yc).astype(BF16)
    o_ref[...] = x_ref[...] + gate_ref[...] * jnp.dot(merged_ref[...], wo_ref[...], preferred_element_type=F32)


def _merge(x, m3, row_fn, norm_g, attn, cu, bg, conv_w, w_gate, wao, wco, wo, seq_len, tm):
    m = x.shape[0]
    row = lambda i: (i, 0)
    const = lambda i: (0, 0)
    resident = functools.partial(pl.BlockSpec, index_map=const, pipeline_mode=pl.Buffered(1))
    assert seq_len % tm == 0 or tm % seq_len == 0
    blocks_per_tile = tm // HALO
    last_halo = m // HALO - 1
    in_specs = [
        pl.BlockSpec((tm, D_MODEL), row),
        _mod_spec(0, row_fn),
        _mod_spec(1, row_fn),
        _mod_spec(2, row_fn),
        pl.BlockSpec((1, D_MODEL), const),
        pl.BlockSpec((tm, Q_WIDTH), row),
        pl.BlockSpec((tm, CONV_WIDTH), row),
        pl.BlockSpec((HALO, CONV_WIDTH), lambda i: (jnp.maximum(i * blocks_per_tile - 1, 0), 0)),
        pl.BlockSpec((HALO, CONV_WIDTH), lambda i: (jnp.minimum((i + 1) * blocks_per_tile, last_halo), 0)),
        pl.BlockSpec((tm, CONV_WIDTH), row),
        pl.BlockSpec((3, CONV_WIDTH), const),
        resident((D_MODEL, 2 * D_MODEL)),
        resident((Q_WIDTH, D_MODEL)),
        resident((CONV_WIDTH, D_MODEL)),
        resident((D_MODEL, D_MODEL)),
    ]
    return pl.pallas_call(
        functools.partial(_merge_kernel, tm=tm, seq_len=seq_len),
        grid=(m // tm,),
        in_specs=in_specs,
        out_specs=pl.BlockSpec((tm, D_MODEL), row),
        out_shape=jax.ShapeDtypeStruct((m, D_MODEL), F32),
        scratch_shapes=[pltpu.VMEM((tm, D_MODEL), BF16), pltpu.VMEM((tm, CONV_WIDTH), BF16),
                        pltpu.VMEM((tm, D_MODEL), BF16)],
        compiler_params=_params(("parallel",), 56),
        name="merge",
    )(x, m3, m3, m3, norm_g, attn, cu, cu, cu, bg, conv_w, w_gate, wao, wco, wo)


def _mlp_kernel(x_ref, sh_ref, sc_ref, gate_ref, g_ref, fg_ref, w1_ref, w2_ref, o_ref, h_ref):
    fc = pl.program_id(1)
    last = pl.num_programs(1) - 1
    tm = x_ref.shape[0]

    def part():
        f = jnp.maximum(jnp.dot(h_ref[...], w1_ref[...], preferred_element_type=F32), 0.0)
        return jnp.dot((f * f).astype(BF16), w2_ref[...], preferred_element_type=F32)

    @pl.when(fc == 0)
    def _():
        _modulated_norm_rows(h_ref, x_ref, g_ref, sc_ref, sh_ref, tm)
        o_ref[...] = part()

    @pl.when((fc > 0) & (fc < last))
    def _():
        o_ref[...] += part()

    @pl.when(fc == last)
    def _():
        o_ref[...] += part()
        gate = gate_ref[...]
        fg = fg_ref[...]
        rows = 16
        for r0 in range(0, tm, rows):
            y = x_ref[r0:r0 + rows, :] + gate * o_ref[r0:r0 + rows, :]
            ms = jnp.mean(y * y, axis=-1, keepdims=True)
            o_ref[r0:r0 + rows, :] = y * lax.rsqrt(ms + EPS) * fg


def _mlp(x, m3, row_fn, norm_g, final_g, w1, w2, tm, tf):
    m = x.shape[0]
    row = lambda i, j: (i, 0)
    const = lambda i, j: (0, 0)
    in_specs = [
        pl.BlockSpec((tm, D_MODEL), row),
        _mod_spec(3, row_fn),
        _mod_spec(4, row_fn),
        _mod_spec(5, row_fn),
        pl.BlockSpec((1, D_MODEL), const),
        pl.BlockSpec((1, D_MODEL), const),
        pl.BlockSpec((D_MODEL, tf), lambda i, j: (0, j)),
        pl.BlockSpec((tf, D_MODEL), lambda i, j: (j, 0)),
    ]
    return pl.pallas_call(
        _mlp_kernel,
        grid=(m // tm, D_FF // tf),
        in_specs=in_specs,
        out_specs=pl.BlockSpec((tm, D_MODEL), row),
        out_shape=jax.ShapeDtypeStruct((m, D_MODEL), F32),
        scratch_shapes=[pltpu.VMEM((tm, D_MODEL), BF16)],
        compiler_params=_params(("parallel", "arbitrary"), 56),
        name="mlp",
    )(x, m3, m3, m3, norm_g, final_g, w1, w2)


def _rope_tables(n_tokens):
    rows = n_tokens // GRID_W
    row = jnp.broadcast_to(jnp.arange(rows)[:, None], (rows, GRID_W)).reshape(-1).astype(F32)
    col = jnp.broadcast_to(jnp.arange(GRID_W)[None, :], (rows, GRID_W)).reshape(-1).astype(F32)
    half = HEAD_DIM // 2
    inv = ROPE_THETA ** (-jnp.arange(0, half, 2, dtype=F32) / half)
    ang = jnp.concatenate([row[:, None] * inv, col[:, None] * inv], axis=-1)
    cos2 = jnp.repeat(jnp.cos(ang), 2, axis=-1)
    sign = jnp.tile(jnp.array([-1.0, 1.0], F32), half)
    sin2 = jnp.repeat(jnp.sin(ang), 2, axis=-1) * sign
    return cos2, sin2


def _stream(x, m3, row_fn, seq_len, rope_tabs, cache, emit_kv, lw, tiles):
    n_batch = x.shape[0]
    xf = x.reshape(n_batch * seq_len, D_MODEL)
    proj = _inproj(xf, m3, functools.partial(row_fn, tm=tiles["inproj"]), lw["norm1_g"], lw["w_proj"], lw["q_gain"],
                   lw["k_gain"], rope_tabs, seq_len, emit_kv, tiles["inproj"])
    if emit_kv:
        q, k, vt, kf, vf, cu, bg = proj
    else:
        q, k, vt, cu, bg = proj
        kf = vf = None
    attn = _attention(q, k, vt, cache, n_batch, seq_len, tiles["tq"])
    x1 = _merge(xf, m3, functools.partial(row_fn, tm=tiles["merge"]), lw["norm1_g"], attn, cu, bg, lw["conv_w"],
                lw["w_gate"], lw["w_attn_out"], lw["w_conv_out"], lw["w_o"], seq_len, tiles["merge"])
    y = _mlp(x1, m3, functools.partial(row_fn, tm=tiles["mlp"]), lw["norm2_g"], lw["final_g"], lw["w_ff1"],
             lw["w_ff2"], tiles["mlp"], tiles["tf"])
    return y.reshape(x.shape), kf, vf


def _ctx_row(i, tm):
    return 0


def _lat_row(i, tm, seq_len):
    return 1 + (i * tm) // seq_len


def kernel(x_prompt, x_sample, cache_k, cache_v, c, c_ctx, w_mod, b_mod, norm1_g, norm2_g, w_in, q_gain, k_gain,
           conv_w, w_attn_out, w_conv_out, w_o, w_ff1, w_ff2, final_g):
    depth = w_mod.shape[0]
    batch, seq, _ = x_prompt.shape
    dec_batch, dec_seq, _ = x_sample.shape
    past = cache_k.shape[2]
    assert depth == 1 and 1 + dec_batch <= COND_ROWS

    cond = jnp.zeros((COND_ROWS, D_MODEL), F32).at[0].set(c_ctx).at[1:1 + dec_batch].set(c)
    rope_tabs = _rope_tables(dec_seq)

    xp, xs = x_prompt, x_sample
    new_ks, new_vs = [], []
    for l in range(depth):
        lw = {
            "norm1_g": norm1_g[l].reshape(1, D_MODEL),
            "norm2_g": norm2_g[l].reshape(1, D_MODEL),
            "final_g": final_g.reshape(1, D_MODEL),
            "q_gain": q_gain[l].reshape(1, HEAD_DIM),
            "k_gain": k_gain[l].reshape(1, HEAD_DIM),
            "conv_w": conv_w[l],
            "w_proj": w_in[l, :, :PROJ_WIDTH].astype(BF16),
            "w_gate": w_in[l, :, PROJ_WIDTH:].astype(BF16),
            "w_attn_out": w_attn_out[l].astype(BF16),
            "w_conv_out": w_conv_out[l].astype(BF16),
            "w_o": w_o[l].astype(BF16),
            "w_ff1": w_ff1[l].astype(BF16),
            "w_ff2": w_ff2[l].astype(BF16),
        }
        m3 = _modulation(cond, w_mod[l], b_mod[l]).reshape(COND_ROWS, 1, N_MOD * D_MODEL)
        cache = (cache_k[:, l].reshape(dec_batch, past, KV_WIDTH), cache_v[:, l].reshape(dec_batch, past, KV_WIDTH))
        ctx_tiles = {"inproj": 512, "tq": seq, "merge": 256, "mlp": 1024, "tf": 512}
        lat_tiles = {"inproj": 512, "tq": 256, "merge": 256, "mlp": 1024, "tf": 512}
        xp, kf, vf = _stream(xp, m3, _ctx_row, seq, None, None, True, lw, ctx_tiles)
        xs, _, _ = _stream(xs, m3, functools.partial(_lat_row, seq_len=dec_seq), dec_seq, rope_tabs, cache, False,
                           lw, lat_tiles)
        new_ks.append(kf.reshape(batch, seq, N_KV_HEADS, HEAD_DIM))
        new_vs.append(vf.reshape(batch, seq, N_KV_HEADS, HEAD_DIM))
    return xp, xs, jnp.stack(new_ks, axis=1), jnp.stack(new_vs, axis=1)
```

```python
import functools

import jax
import jax.numpy as jnp
from jax import lax
from jax.experimental import pallas as pl
from jax.experimental.pallas import tpu as pltpu

D_MODEL = 2048
GRID_W = 64
HEAD_DIM = 128
N_HEADS = 8
N_KV_HEADS = 2
GROUPS = N_HEADS // N_KV_HEADS
Q_WIDTH = N_HEADS * HEAD_DIM
KV_WIDTH = N_KV_HEADS * HEAD_DIM
CONV_WIDTH = D_MODEL // 2
D_FF = 4 * D_MODEL
ROPE_THETA = 10000.0
EPS = 1e-6
N_MOD = 6
LOG2_E = 1.4426950408889634
COND_ROWS = 16
HALO = 8

OFF_Q = 0
OFF_K = Q_WIDTH
OFF_V = OFF_K + KV_WIDTH
OFF_U = OFF_V + KV_WIDTH
OFF_C = OFF_U + CONV_WIDTH
OFF_B = OFF_C + CONV_WIDTH
OFF_A = OFF_B + CONV_WIDTH
OFF_S = OFF_A + D_MODEL
IN_WIDTH = OFF_S + D_MODEL
PROJ_WIDTH = OFF_A

MIB = 1024 * 1024
F32 = jnp.float32
BF16 = jnp.bfloat16


def _params(sem, vmem_mib):
    return pltpu.CompilerParams(dimension_semantics=sem, vmem_limit_bytes=vmem_mib * MIB)


def _mod_spec(chunk, row_fn):
    return pl.BlockSpec((None, 1, D_MODEL), lambda i, *_: (row_fn(i), 0, chunk))


def _modulated_norm(x, g, sc, sh):
    ms = jnp.mean(x * x, axis=-1, keepdims=True)
    y = x * lax.rsqrt(ms + EPS) * g
    return y * (1.0 + sc) + sh


def _modulated_norm_rows(h_ref, x_ref, g_ref, sc_ref, sh_ref, tm):
    rows = 16
    gs = g_ref[...] * (1.0 + sc_ref[...])
    sh = sh_ref[...]
    for r0 in range(0, tm, rows):
        x = x_ref[r0:r0 + rows, :]
        ms = jnp.mean(x * x, axis=-1, keepdims=True)
        h_ref[r0:r0 + rows, :] = (x * lax.rsqrt(ms + EPS) * gs + sh).astype(BF16)


def _head_norm(z, gain):
    ms = jnp.mean(z * z, axis=-1, keepdims=True)
    return z * lax.rsqrt(ms + EPS) * gain


def _rope(x, cos2, sin2):
    lane = lax.broadcasted_iota(jnp.int32, x.shape, 1)
    nxt = pltpu.roll(x, HEAD_DIM - 1, axis=1)
    prv = pltpu.roll(x, 1, axis=1)
    swapped = jnp.where(lane % 2 == 0, nxt, prv)
    return x * cos2 + swapped * sin2


def _mod_kernel(cond_ref, w_ref, b_ref, o_ref):
    cond = cond_ref[...]
    a = (cond * jax.nn.sigmoid(cond)).astype(BF16)
    o_ref[...] = jnp.dot(a, w_ref[...].astype(BF16), preferred_element_type=F32) + b_ref[...]


def _modulation(cond, w_mod, b_mod):
    tn = 1024
    n = w_mod.shape[1]
    return pl.pallas_call(
        _mod_kernel,
        grid=(n // tn,),
        in_specs=[
            pl.BlockSpec((COND_ROWS, D_MODEL), lambda j: (0, 0)),
            pl.BlockSpec((D_MODEL, tn), lambda j: (0, j)),
            pl.BlockSpec((1, tn), lambda j: (0, j)),
        ],
        out_specs=pl.BlockSpec((COND_ROWS, tn), lambda j: (0, j)),
        out_shape=jax.ShapeDtypeStruct((COND_ROWS, n), F32),
        compiler_params=_params(("arbitrary",), 40),
        name="modulation",
    )(cond, w_mod, b_mod.reshape(1, n))


def _inproj_kernel(*refs, rope, emit_kv):
    x_ref, sh_ref, sc_ref, g_ref, w_ref, qg_ref, kg_ref = refs[:7]
    pos = 7
    if rope:
        cos_ref, sin_ref = refs[pos:pos + 2]
        pos += 2
    q_ref, k_ref, vt_ref = refs[pos:pos + 3]
    pos += 3
    if emit_kv:
        kf_ref, vf_ref = refs[pos:pos + 2]
        pos += 2
    cu_ref, bg_ref, h_ref = refs[pos:pos + 3]

    _modulated_norm_rows(h_ref, x_ref, g_ref, sc_ref, sh_ref, x_ref.shape[0])

    def proj(off, width):
        return jnp.dot(h_ref[...], w_ref[:, off:off + width], preferred_element_type=F32)

    if rope:
        cos2 = cos_ref[...]
        sin2 = sin_ref[...]
    scale = HEAD_DIM ** -0.5 * LOG2_E

    for hd in range(N_HEADS // 4):
        z = proj(OFF_Q + hd * 4 * HEAD_DIM, 4 * HEAD_DIM)
        for hh in range(4):
            qn = _head_norm(z[:, hh * HEAD_DIM:(hh + 1) * HEAD_DIM], qg_ref[...])
            if rope:
                qn = _rope(qn, cos2, sin2)
            col = (hd * 4 + hh) * HEAD_DIM
            q_ref[:, col:col + HEAD_DIM] = (qn * scale).astype(BF16)

    z = proj(OFF_K, 2 * KV_WIDTH)
    for hh in range(N_KV_HEADS):
        col = hh * HEAD_DIM
        kn = _head_norm(z[:, col:col + HEAD_DIM], kg_ref[...])
        if emit_kv:
            kf_ref[:, col:col + HEAD_DIM] = kn
        if rope:
            kn = _rope(kn, cos2, sin2)
        k_ref[:, col:col + HEAD_DIM] = kn.astype(BF16)
    vv = z[:, KV_WIDTH:]
    if emit_kv:
        vf_ref[...] = vv
    vt_ref[...] = vv.T.astype(BF16)

    cw = 256
    for c in range(CONV_WIDTH // cw):
        u = proj(OFF_U + c * cw, cw)
        gc = proj(OFF_C + c * cw, cw)
        cu_ref[:, c * cw:(c + 1) * cw] = gc * u

    bw = 512
    for c in range(CONV_WIDTH // bw):
        bg_ref[:, c * bw:(c + 1) * bw] = proj(OFF_B + c * bw, bw)


def _inproj(x, m3, row_fn, norm_g, w_in_bf, q_gain, k_gain, rope_tabs, seq_len, emit_kv, tm):
    m = x.shape[0]
    rope = rope_tabs is not None
    row = lambda i: (i, 0)
    const = lambda i: (0, 0)
    in_specs = [
        pl.BlockSpec((tm, D_MODEL), row),
        _mod_spec(0, row_fn),
        _mod_spec(1, row_fn),
        pl.BlockSpec((1, D_MODEL), const),
        pl.BlockSpec((D_MODEL, PROJ_WIDTH), const, pipeline_mode=pl.Buffered(1)),
        pl.BlockSpec((1, HEAD_DIM), const),
        pl.BlockSpec((1, HEAD_DIM), const),
    ]
    args = [x, m3, m3, norm_g, w_in_bf, q_gain, k_gain]
    if rope:
        tiles_per_seq = seq_len // tm
        tab_spec = pl.BlockSpec((tm, HEAD_DIM), lambda i: (i % tiles_per_seq, 0))
        in_specs += [tab_spec, tab_spec]
        args += list(rope_tabs)
    out_specs = [
        pl.BlockSpec((tm, Q_WIDTH), row),
        pl.BlockSpec((tm, KV_WIDTH), row),
        pl.BlockSpec((KV_WIDTH, tm), lambda i: (0, i)),
    ]
    out_shape = [
        jax.ShapeDtypeStruct((m, Q_WIDTH), BF16),
        jax.ShapeDtypeStruct((m, KV_WIDTH), BF16),
        jax.ShapeDtypeStruct((KV_WIDTH, m), BF16),
    ]
    if emit_kv:
        out_specs += [pl.BlockSpec((tm, KV_WIDTH), row)] * 2
        out_shape += [jax.ShapeDtypeStruct((m, KV_WIDTH), F32)] * 2
    out_specs += [pl.BlockSpec((tm, CONV_WIDTH), row)] * 2
    out_shape += [jax.ShapeDtypeStruct((m, CONV_WIDTH), F32)] * 2
    return pl.pallas_call(
        functools.partial(_inproj_kernel, rope=rope, emit_kv=emit_kv),
        grid=(m // tm,),
        in_specs=in_specs,
        out_specs=out_specs,
        out_shape=out_shape,
        scratch_shapes=[pltpu.VMEM((tm, D_MODEL), BF16)],
        compiler_params=_params(("parallel",), 56),
        name="inproj_lat" if rope else "inproj_ctx",
    )(*args)


def _attn_kernel(*refs, tq, tk, n_tiles, has_cache):
    if has_cache:
        q_ref, k_ref, vt_ref, ck_ref, cv_ref, o_ref, s0_ref, s1_ref, m0_ref, m1_ref = refs
        past = ck_ref.shape[0]
    else:
        q_ref, k_ref, vt_ref, o_ref, s0_ref, s1_ref, m0_ref, m1_ref = refs
        past = 0
    t = pl.program_id(0)
    nt = (((1,), (1,)), ((), ()))

    seq_len = k_ref.shape[0]
    chunks = ([(0, past)] if has_cache else []) + [(past + c, tk) for c in range(0, seq_len, tk)]

    def step(score_slot, value_slot):
        if score_slot is not None:
            s_w, m_w = score_slot
            qs = jnp.concatenate([q_ref[:, g * HEAD_DIM:(g + 1) * HEAD_DIM] for g in range(GROUPS)], axis=0)
        if value_slot is not None:
            s_r, m_r = value_slot
            m_old = m_r[...]
        m_new = l_sum = acc = None
        for off, n in chunks:
            cached = off < past
            if score_slot is not None:
                kc = ck_ref[...].astype(BF16) if cached else k_ref[off - past:off - past + n, :]
                st = lax.dot_general(kc, qs, nt, preferred_element_type=F32)
                s_w[off:off + n, :] = st
                cm = jnp.max(st, axis=0, keepdims=True)
                m_new = cm if m_new is None else jnp.maximum(m_new, cm)
            if value_slot is not None:
                p = jnp.exp2(s_r[off:off + n, :] - m_old)
                ps = jnp.sum(p, axis=0, keepdims=True)
                l_sum = ps if l_sum is None else l_sum + ps
                vtc = cv_ref[...].T.astype(BF16) if cached else vt_ref[:, off - past:off - past + n]
                pv = jnp.dot(vtc, p.astype(BF16), preferred_element_type=F32)
                acc = pv if acc is None else acc + pv
        if score_slot is not None:
            m_w[...] = m_new
        if value_slot is not None:
            out_t = acc / l_sum
            for g in range(GROUPS):
                o_ref[:, g * HEAD_DIM:(g + 1) * HEAD_DIM] = out_t[:, g * tq:(g + 1) * tq].T.astype(BF16)

    slots = ((s0_ref, m0_ref), (s1_ref, m1_ref))

    @pl.when(t == 0)
    def _():
        step(slots[0], None)

    for parity in range(2):
        @pl.when((t > 0) & (t < n_tiles) & (t % 2 == parity))
        def _():
            step(slots[parity], slots[1 - parity])

    @pl.when(t == n_tiles)
    def _():
        step(None, slots[(n_tiles - 1) % 2])


def _attention(q, k, vt, cache, n_batch, seq_len, tq, tk):
    m = q.shape[0]
    nq = seq_len // tq
    n_tiles = n_batch * N_KV_HEADS * nq
    has_cache = cache is not None

    def tile(u):
        return u // (N_KV_HEADS * nq), (u // nq) % N_KV_HEADS, u % nq

    def score_tile(t):
        return tile(jnp.minimum(t, n_tiles - 1))

    def value_tile(t):
        return tile(jnp.maximum(t - 1, 0))

    def q_map(t):
        b, h, i = score_tile(t)
        return b * nq + i, h

    def k_map(t):
        b, h, _ = score_tile(t)
        return b, h

    def ck_map(t):
        b, h, _ = score_tile(t)
        return b, 0, h

    def vt_map(t):
        b, h, _ = value_tile(t)
        return h, b

    def cv_map(t):
        b, h, _ = value_tile(t)
        return b, 0, h

    def o_map(t):
        b, h, i = value_tile(t)
        return b * nq + i, h

    in_specs = [
        pl.BlockSpec((tq, GROUPS * HEAD_DIM), q_map),
        pl.BlockSpec((seq_len, HEAD_DIM), k_map),
        pl.BlockSpec((HEAD_DIM, seq_len), vt_map),
    ]
    args = [q, k, vt]
    past = 0
    if has_cache:
        past = cache[0].shape[1]
        in_specs += [pl.BlockSpec((None, past, HEAD_DIM), ck_map), pl.BlockSpec((None, past, HEAD_DIM), cv_map)]
        args += list(cache)
    cols = GROUPS * tq
    score_buf = pltpu.VMEM((past + seq_len, cols), F32)
    max_buf = pltpu.VMEM((1, cols), F32)
    return pl.pallas_call(
        functools.partial(_attn_kernel, tq=tq, tk=tk, n_tiles=n_tiles, has_cache=has_cache),
        scratch_shapes=[score_buf, score_buf, max_buf, max_buf],
        grid=(n_tiles + 1,),
        in_specs=in_specs,
        out_specs=pl.BlockSpec((tq, GROUPS * HEAD_DIM), o_map),
        out_shape=jax.ShapeDtypeStruct((m, Q_WIDTH), BF16),
        compiler_params=_params(("arbitrary",), 56),
        name="attn_lat" if has_cache else "attn_ctx",
    )(*args)


def _merge_kernel(x_ref, sh_ref, sc_ref, gate_ref, g_ref, attn_ref, cu_ref, cup_ref, cun_ref, bg_ref, cw_ref,
                  wg_ref, wao_ref, wco_ref, wo_ref, o_ref, h_ref, conv_ref, merged_ref, *, tm, seq_len):

    def conv_rows():
        rows = 32
        w0, w1, w2 = cw_ref[0:1, :], cw_ref[1:2, :], cw_ref[2:3, :]
        zeros = jnp.zeros((HALO, CONV_WIDTH), F32)
        if seq_len > tm:
            tiles_per_seq = seq_len // tm
            tile_in_seq = jnp.full((HALO, CONV_WIDTH), pl.program_id(0) % tiles_per_seq, jnp.int32)
            first_halo = jnp.where(tile_in_seq == 0, 0.0, cup_ref[...])
            last_halo = jnp.where(tile_in_seq == tiles_per_seq - 1, 0.0, cun_ref[...])
        else:
            first_halo = last_halo = zeros
        for r0 in range(0, tm, rows):
            end = r0 + rows
            cur = cu_ref[r0:end, :]
            if r0 == 0:
                prev = first_halo
            elif r0 % seq_len == 0:
                prev = zeros
            else:
                prev = cu_ref[r0 - HALO:r0, :]
            if end == tm:
                nxt = last_halo
            elif end % seq_len == 0:
                nxt = zeros
            else:
                nxt = cu_ref[end:end + HALO, :]
            ext = jnp.concatenate([prev, cur, nxt], axis=0)
            up = pltpu.roll(ext, 1, axis=0)[HALO:HALO + rows]
            dn = pltpu.roll(ext, rows + 2 * HALO - 1, axis=0)[HALO:HALO + rows]
            conv = up * w0 + cur * w1 + dn * w2
            conv_ref[r0:end, :] = (bg_ref[r0:end, :] * conv).astype(BF16)

    _modulated_norm_rows(h_ref, x_ref, g_ref, sc_ref, sh_ref, tm)
    conv_rows()
    tn = 512
    for c0 in range(0, D_MODEL, tn):
        ya = jnp.dot(attn_ref[...], wao_ref[:, c0:c0 + tn], preferred_element_type=F32)
        ga = jax.nn.sigmoid(jnp.dot(h_ref[...], wg_ref[:, c0:c0 + tn], preferred_element_type=F32))
        gs = jax.nn.sigmoid(jnp.dot(h_ref[...], wg_ref[:, D_MODEL + c0:D_MODEL + c0 + tn], preferred_element_type=F32))
        yc = jnp.dot(conv_ref[...], wco_ref[:, c0:c0 + tn], preferred_element_type=F32)
        merged_ref[:, c0:c0 + tn] = (ga * ya + gs * yc).astype(BF16)
    o_ref[...] = x_ref[...] + gate_ref[...] * jnp.dot(merged_ref[...], wo_ref[...], preferred_element_type=F32)


def _merge(x, m3, row_fn, norm_g, attn, cu, bg, conv_w, w_gate, wao, wco, wo, seq_len, tm):
    m = x.shape[0]
    row = lambda i: (i, 0)
    const = lambda i: (0, 0)
    resident = functools.partial(pl.BlockSpec, index_map=const, pipeline_mode=pl.Buffered(1))
    assert seq_len % tm == 0 or tm % seq_len == 0
    blocks_per_tile = tm // HALO
    last_halo = m // HALO - 1
    in_specs = [
        pl.BlockSpec((tm, D_MODEL), row),
        _mod_spec(0, row_fn),
        _mod_spec(1, row_fn),
        _mod_spec(2, row_fn),
        pl.BlockSpec((1, D_MODEL), const),
        pl.BlockSpec((tm, Q_WIDTH), row),
        pl.BlockSpec((tm, CONV_WIDTH), row),
        pl.BlockSpec((HALO, CONV_WIDTH), lambda i: (jnp.maximum(i * blocks_per_tile - 1, 0), 0)),
        pl.BlockSpec((HALO, CONV_WIDTH), lambda i: (jnp.minimum((i + 1) * blocks_per_tile, last_halo), 0)),
        pl.BlockSpec((tm, CONV_WIDTH), row),
        pl.BlockSpec((3, CONV_WIDTH), const),
        resident((D_MODEL, 2 * D_MODEL)),
        resident((Q_WIDTH, D_MODEL)),
        resident((CONV_WIDTH, D_MODEL)),
        resident((D_MODEL, D_MODEL)),
    ]
    return pl.pallas_call(
        functools.partial(_merge_kernel, tm=tm, seq_len=seq_len),
        grid=(m // tm,),
        in_specs=in_specs,
        out_specs=pl.BlockSpec((tm, D_MODEL), row),
        out_shape=jax.ShapeDtypeStruct((m, D_MODEL), F32),
        scratch_shapes=[pltpu.VMEM((tm, D_MODEL), BF16), pltpu.VMEM((tm, CONV_WIDTH), BF16),
                        pltpu.VMEM((tm, D_MODEL), BF16)],
        compiler_params=_params(("parallel",), 56),
        name="merge",
    )(x, m3, m3, m3, norm_g, attn, cu, cu, cu, bg, conv_w, w_gate, wao, wco, wo)


def _mlp_kernel(x_ref, sh_ref, sc_ref, gate_ref, g_ref, fg_ref, w1_ref, w2_ref, o_ref, h_ref):
    fc = pl.program_id(1)
    last = pl.num_programs(1) - 1
    tm = x_ref.shape[0]

    def part():
        f = jnp.maximum(jnp.dot(h_ref[...], w1_ref[...], preferred_element_type=F32), 0.0)
        return jnp.dot((f * f).astype(BF16), w2_ref[...], preferred_element_type=F32)

    @pl.when(fc == 0)
    def _():
        _modulated_norm_rows(h_ref, x_ref, g_ref, sc_ref, sh_ref, tm)
        o_ref[...] = part()

    @pl.when((fc > 0) & (fc < last))
    def _():
        o_ref[...] += part()

    @pl.when(fc == last)
    def _():
        o_ref[...] += part()
        gate = gate_ref[...]
        fg = fg_ref[...]
        rows = 16
        for r0 in range(0, tm, rows):
            y = x_ref[r0:r0 + rows, :] + gate * o_ref[r0:r0 + rows, :]
            ms = jnp.mean(y * y, axis=-1, keepdims=True)
            o_ref[r0:r0 + rows, :] = y * lax.rsqrt(ms + EPS) * fg


def _mlp(x, m3, row_fn, norm_g, final_g, w1, w2, tm, tf):
    m = x.shape[0]
    row = lambda i, j: (i, 0)
    const = lambda i, j: (0, 0)
    in_specs = [
        pl.BlockSpec((tm, D_MODEL), row),
        _mod_spec(3, row_fn),
        _mod_spec(4, row_fn),
        _mod_spec(5, row_fn),
        pl.BlockSpec((1, D_MODEL), const),
        pl.BlockSpec((1, D_MODEL), const),
        pl.BlockSpec((D_MODEL, tf), lambda i, j: (0, j)),
        pl.BlockSpec((tf, D_MODEL), lambda i, j: (j, 0)),
    ]
    return pl.pallas_call(
        _mlp_kernel,
        grid=(m // tm, D_FF // tf),
        in_specs=in_specs,
        out_specs=pl.BlockSpec((tm, D_MODEL), row),
        out_shape=jax.ShapeDtypeStruct((m, D_MODEL), F32),
        scratch_shapes=[pltpu.VMEM((tm, D_MODEL), BF16)],
        compiler_params=_params(("parallel", "arbitrary"), 56),
        name="mlp",
    )(x, m3, m3, m3, norm_g, final_g, w1, w2)


def _rope_tables(n_tokens):
    rows = n_tokens // GRID_W
    row = jnp.broadcast_to(jnp.arange(rows)[:, None], (rows, GRID_W)).reshape(-1).astype(F32)
    col = jnp.broadcast_to(jnp.arange(GRID_W)[None, :], (rows, GRID_W)).reshape(-1).astype(F32)
    half = HEAD_DIM // 2
    inv = ROPE_THETA ** (-jnp.arange(0, half, 2, dtype=F32) / half)
    ang = jnp.concatenate([row[:, None] * inv, col[:, None] * inv], axis=-1)
    cos2 = jnp.repeat(jnp.cos(ang), 2, axis=-1)
    sign = jnp.tile(jnp.array([-1.0, 1.0], F32), half)
    sin2 = jnp.repeat(jnp.sin(ang), 2, axis=-1) * sign
    return cos2, sin2


def _stream(x, m3, row_fn, seq_len, rope_tabs, cache, emit_kv, lw, tiles):
    n_batch = x.shape[0]
    xf = x.reshape(n_batch * seq_len, D_MODEL)
    proj = _inproj(xf, m3, functools.partial(row_fn, tm=tiles["inproj"]), lw["norm1_g"], lw["w_proj"], lw["q_gain"],
                   lw["k_gain"], rope_tabs, seq_len, emit_kv, tiles["inproj"])
    if emit_kv:
        q, k, vt, kf, vf, cu, bg = proj
    else:
        q, k, vt, cu, bg = proj
        kf = vf = None
    attn = _attention(q, k, vt, cache, n_batch, seq_len, tiles["tq"], tiles["tk"])
    x1 = _merge(xf, m3, functools.partial(row_fn, tm=tiles["merge"]), lw["norm1_g"], attn, cu, bg, lw["conv_w"],
                lw["w_gate"], lw["w_attn_out"], lw["w_conv_out"], lw["w_o"], seq_len, tiles["merge"])
    y = _mlp(x1, m3, functools.partial(row_fn, tm=tiles["mlp"]), lw["norm2_g"], lw["final_g"], lw["w_ff1"],
             lw["w_ff2"], tiles["mlp"], tiles["tf"])
    return y.reshape(x.shape), kf, vf


def _ctx_row(i, tm):
    return 0


def _lat_row(i, tm, seq_len):
    return 1 + (i * tm) // seq_len


def kernel(x_prompt, x_sample, cache_k, cache_v, c, c_ctx, w_mod, b_mod, norm1_g, norm2_g, w_in, q_gain, k_gain,
           conv_w, w_attn_out, w_conv_out, w_o, w_ff1, w_ff2, final_g):
    depth = w_mod.shape[0]
    batch, seq, _ = x_prompt.shape
    dec_batch, dec_seq, _ = x_sample.shape
    past = cache_k.shape[2]
    assert depth == 1 and 1 + dec_batch <= COND_ROWS

    cond = jnp.zeros((COND_ROWS, D_MODEL), F32).at[0].set(c_ctx).at[1:1 + dec_batch].set(c)
    rope_tabs = _rope_tables(dec_seq)

    xp, xs = x_prompt, x_sample
    new_ks, new_vs = [], []
    for l in range(depth):
        lw = {
            "norm1_g": norm1_g[l].reshape(1, D_MODEL),
            "norm2_g": norm2_g[l].reshape(1, D_MODEL),
            "final_g": final_g.reshape(1, D_MODEL),
            "q_gain": q_gain[l].reshape(1, HEAD_DIM),
            "k_gain": k_gain[l].reshape(1, HEAD_DIM),
            "conv_w": conv_w[l],
            "w_proj": w_in[l, :, :PROJ_WIDTH].astype(BF16),
            "w_gate": w_in[l, :, PROJ_WIDTH:].astype(BF16),
            "w_attn_out": w_attn_out[l].astype(BF16),
            "w_conv_out": w_conv_out[l].astype(BF16),
            "w_o": w_o[l].astype(BF16),
            "w_ff1": w_ff1[l].astype(BF16),
            "w_ff2": w_ff2[l].astype(BF16),
        }
        m3 = _modulation(cond, w_mod[l], b_mod[l]).reshape(COND_ROWS, 1, N_MOD * D_MODEL)
        cache = (cache_k[:, l].reshape(dec_batch, past, KV_WIDTH), cache_v[:, l].reshape(dec_batch, past, KV_WIDTH))
        ctx_tiles = {"inproj": 512, "tq": seq, "tk": seq, "merge": 256, "mlp": 1024, "tf": 512}
        lat_tiles = {"inproj": 512, "tq": 256, "tk": 256, "merge": 256, "mlp": 1024, "tf": 512}
        xp, kf, vf = _stream(xp, m3, _ctx_row, seq, None, None, True, lw, ctx_tiles)
        xs, _, _ = _stream(xs, m3, functools.partial(_lat_row, seq_len=dec_seq), dec_seq, rope_tabs, cache, False,
                           lw, lat_tiles)
        new_ks.append(kf.reshape(batch, seq, N_KV_HEADS, HEAD_DIM))
        new_vs.append(vf.reshape(batch, seq, N_KV_HEADS, HEAD_DIM))
    return xp, xs, jnp.stack(new_ks, axis=1), jnp.stack(new_vs, axis=1)
```

```python
import functools

import jax
import jax.numpy as jnp
from jax import lax
from jax.experimental import pallas as pl
from jax.experimental.pallas import tpu as pltpu

D_MODEL = 2048
GRID_W = 64
HEAD_DIM = 128
N_HEADS = 8
N_KV_HEADS = 2
GROUPS = N_HEADS // N_KV_HEADS
Q_WIDTH = N_HEADS * HEAD_DIM
KV_WIDTH = N_KV_HEADS * HEAD_DIM
CONV_WIDTH = D_MODEL // 2
D_FF = 4 * D_MODEL
ROPE_THETA = 10000.0
EPS = 1e-6
N_MOD = 6
LOG2_E = 1.4426950408889634
COND_ROWS = 16
HALO = 8

OFF_Q = 0
OFF_K = Q_WIDTH
OFF_V = OFF_K + KV_WIDTH
OFF_U = OFF_V + KV_WIDTH
OFF_C = OFF_U + CONV_WIDTH
OFF_B = OFF_C + CONV_WIDTH
OFF_A = OFF_B + CONV_WIDTH
OFF_S = OFF_A + D_MODEL
IN_WIDTH = OFF_S + D_MODEL
PROJ_WIDTH = OFF_A

MIB = 1024 * 1024
F32 = jnp.float32
BF16 = jnp.bfloat16


def _params(sem, vmem_mib):
    return pltpu.CompilerParams(dimension_semantics=sem, vmem_limit_bytes=vmem_mib * MIB)


def _mod_spec(chunk, row_fn):
    return pl.BlockSpec((None, 1, D_MODEL), lambda i, *_: (row_fn(i), 0, chunk))


def _modulated_norm(x, g, sc, sh):
    ms = jnp.mean(x * x, axis=-1, keepdims=True)
    y = x * lax.rsqrt(ms + EPS) * g
    return y * (1.0 + sc) + sh


def _modulated_norm_rows(h_ref, x_ref, g_ref, sc_ref, sh_ref, tm):
    rows = 16
    gs = g_ref[...] * (1.0 + sc_ref[...])
    sh = sh_ref[...]
    for r0 in range(0, tm, rows):
        x = x_ref[r0:r0 + rows, :]
        ms = jnp.mean(x * x, axis=-1, keepdims=True)
        h_ref[r0:r0 + rows, :] = (x * lax.rsqrt(ms + EPS) * gs + sh).astype(BF16)


def _head_norm(z, gain):
    ms = jnp.mean(z * z, axis=-1, keepdims=True)
    return z * lax.rsqrt(ms + EPS) * gain


def _rope(x, cos2, sin2):
    lane = lax.broadcasted_iota(jnp.int32, x.shape, 1)
    nxt = pltpu.roll(x, HEAD_DIM - 1, axis=1)
    prv = pltpu.roll(x, 1, axis=1)
    swapped = jnp.where(lane % 2 == 0, nxt, prv)
    return x * cos2 + swapped * sin2


def _mod_kernel(cond_ref, w_ref, b_ref, o_ref):
    cond = cond_ref[...]
    a = (cond * jax.nn.sigmoid(cond)).astype(BF16)
    o_ref[...] = jnp.dot(a, w_ref[...].astype(BF16), preferred_element_type=F32) + b_ref[...]


def _modulation(cond, w_mod, b_mod):
    tn = 1024
    n = w_mod.shape[1]
    return pl.pallas_call(
        _mod_kernel,
        grid=(n // tn,),
        in_specs=[
            pl.BlockSpec((COND_ROWS, D_MODEL), lambda j: (0, 0)),
            pl.BlockSpec((D_MODEL, tn), lambda j: (0, j)),
            pl.BlockSpec((1, tn), lambda j: (0, j)),
        ],
        out_specs=pl.BlockSpec((COND_ROWS, tn), lambda j: (0, j)),
        out_shape=jax.ShapeDtypeStruct((COND_ROWS, n), F32),
        compiler_params=_params(("arbitrary",), 40),
        name="modulation",
    )(cond, w_mod, b_mod.reshape(1, n))


def _inproj_kernel(*refs, rope, emit_kv):
    x_ref, sh_ref, sc_ref, g_ref, w_ref, qg_ref, kg_ref = refs[:7]
    pos = 7
    if rope:
        cos_ref, sin_ref = refs[pos:pos + 2]
        pos += 2
    q_ref, k_ref, vt_ref = refs[pos:pos + 3]
    pos += 3
    if emit_kv:
        kf_ref, vf_ref = refs[pos:pos + 2]
        pos += 2
    cu_ref, bg_ref, h_ref = refs[pos:pos + 3]

    _modulated_norm_rows(h_ref, x_ref, g_ref, sc_ref, sh_ref, x_ref.shape[0])

    def proj(off, width):
        return jnp.dot(h_ref[...], w_ref[:, off:off + width], preferred_element_type=F32)

    if rope:
        cos2 = cos_ref[...]
        sin2 = sin_ref[...]
    scale = HEAD_DIM ** -0.5 * LOG2_E

    for hd in range(N_HEADS // 4):
        z = proj(OFF_Q + hd * 4 * HEAD_DIM, 4 * HEAD_DIM)
        for hh in range(4):
            qn = _head_norm(z[:, hh * HEAD_DIM:(hh + 1) * HEAD_DIM], qg_ref[...])
            if rope:
                qn = _rope(qn, cos2, sin2)
            col = (hd * 4 + hh) * HEAD_DIM
            q_ref[:, col:col + HEAD_DIM] = (qn * scale).astype(BF16)

    z = proj(OFF_K, 2 * KV_WIDTH)
    for hh in range(N_KV_HEADS):
        col = hh * HEAD_DIM
        kn = _head_norm(z[:, col:col + HEAD_DIM], kg_ref[...])
        if emit_kv:
            kf_ref[:, col:col + HEAD_DIM] = kn
        if rope:
            kn = _rope(kn, cos2, sin2)
        k_ref[:, col:col + HEAD_DIM] = kn.astype(BF16)
    vv = z[:, KV_WIDTH:]
    if emit_kv:
        vf_ref[...] = vv
    vt_ref[...] = vv.T.astype(BF16)

    cw = 256
    for c in range(CONV_WIDTH // cw):
        u = proj(OFF_U + c * cw, cw)
        gc = proj(OFF_C + c * cw, cw)
        cu_ref[:, c * cw:(c + 1) * cw] = gc * u

    bw = 512
    for c in range(CONV_WIDTH // bw):
        bg_ref[:, c * bw:(c + 1) * bw] = proj(OFF_B + c * bw, bw)


def _inproj(x, m3, row_fn, norm_g, w_in_bf, q_gain, k_gain, rope_tabs, seq_len, emit_kv, tm):
    m = x.shape[0]
    rope = rope_tabs is not None
    row = lambda i: (i, 0)
    const = lambda i: (0, 0)
    in_specs = [
        pl.BlockSpec((tm, D_MODEL), row),
        _mod_spec(0, row_fn),
        _mod_spec(1, row_fn),
        pl.BlockSpec((1, D_MODEL), const),
        pl.BlockSpec((D_MODEL, PROJ_WIDTH), const, pipeline_mode=pl.Buffered(1)),
        pl.BlockSpec((1, HEAD_DIM), const),
        pl.BlockSpec((1, HEAD_DIM), const),
    ]
    args = [x, m3, m3, norm_g, w_in_bf, q_gain, k_gain]
    if rope:
        tiles_per_seq = seq_len // tm
        tab_spec = pl.BlockSpec((tm, HEAD_DIM), lambda i: (i % tiles_per_seq, 0))
        in_specs += [tab_spec, tab_spec]
        args += list(rope_tabs)
    out_specs = [
        pl.BlockSpec((tm, Q_WIDTH), row),
        pl.BlockSpec((tm, KV_WIDTH), row),
        pl.BlockSpec((KV_WIDTH, tm), lambda i: (0, i)),
    ]
    out_shape = [
        jax.ShapeDtypeStruct((m, Q_WIDTH), BF16),
        jax.ShapeDtypeStruct((m, KV_WIDTH), BF16),
        jax.ShapeDtypeStruct((KV_WIDTH, m), BF16),
    ]
    if emit_kv:
        out_specs += [pl.BlockSpec((tm, KV_WIDTH), row)] * 2
        out_shape += [jax.ShapeDtypeStruct((m, KV_WIDTH), F32)] * 2
    out_specs += [pl.BlockSpec((tm, CONV_WIDTH), row)] * 2
    out_shape += [jax.ShapeDtypeStruct((m, CONV_WIDTH), F32)] * 2
    return pl.pallas_call(
        functools.partial(_inproj_kernel, rope=rope, emit_kv=emit_kv),
        grid=(m // tm,),
        in_specs=in_specs,
        out_specs=out_specs,
        out_shape=out_shape,
        scratch_shapes=[pltpu.VMEM((tm, D_MODEL), BF16)],
        compiler_params=_params(("parallel",), 56),
        name="inproj_lat" if rope else "inproj_ctx",
    )(*args)


def _attn_kernel(*refs, tq, tk, n_tiles, has_cache):
    if has_cache:
        q_ref, k_ref, vt_ref, ck_ref, cv_ref, o_ref, s0_ref, s1_ref, m0_ref, m1_ref = refs
        past = ck_ref.shape[0]
    else:
        q_ref, k_ref, vt_ref, o_ref, s0_ref, s1_ref, m0_ref, m1_ref = refs
        past = 0
    t = pl.program_id(0)
    nt = (((1,), (1,)), ((), ()))

    seq_len = k_ref.shape[0]
    chunks = ([(0, past)] if has_cache else []) + [(past + c, tk) for c in range(0, seq_len, tk)]

    def step(score_slot, value_slot):
        if score_slot is not None:
            s_w, m_w = score_slot
            qs = jnp.concatenate([q_ref[:, g * HEAD_DIM:(g + 1) * HEAD_DIM] for g in range(GROUPS)], axis=0)
        if value_slot is not None:
            s_r, m_r = value_slot
            m_old = m_r[...]
        m_new = l_sum = acc = None
        for off, n in chunks:
            cached = off < past
            if score_slot is not None:
                kc = ck_ref[...].astype(BF16) if cached else k_ref[off - past:off - past + n, :]
                st = lax.dot_general(kc, qs, nt, preferred_element_type=F32)
                s_w[off:off + n, :] = st
                cm = jnp.max(st, axis=0, keepdims=True)
                m_new = cm if m_new is None else jnp.maximum(m_new, cm)
            if value_slot is not None:
                p = jnp.exp2(s_r[off:off + n, :] - m_old)
                ps = jnp.sum(p, axis=0, keepdims=True)
                l_sum = ps if l_sum is None else l_sum + ps
                vtc = cv_ref[...].T.astype(BF16) if cached else vt_ref[:, off - past:off - past + n]
                pv = jnp.dot(vtc, p.astype(BF16), preferred_element_type=F32)
                acc = pv if acc is None else acc + pv
        if score_slot is not None:
            m_w[...] = m_new
        if value_slot is not None:
            out_t = acc / l_sum
            for g in range(GROUPS):
                o_ref[:, g * HEAD_DIM:(g + 1) * HEAD_DIM] = out_t[:, g * tq:(g + 1) * tq].T.astype(BF16)

    slots = ((s0_ref, m0_ref), (s1_ref, m1_ref))

    @pl.when(t == 0)
    def _():
        step(slots[0], None)

    for parity in range(2):
        @pl.when((t > 0) & (t < n_tiles) & (t % 2 == parity))
        def _():
            step(slots[parity], slots[1 - parity])

    @pl.when(t == n_tiles)
    def _():
        step(None, slots[(n_tiles - 1) % 2])


def _attention(q, k, vt, cache, n_batch, seq_len, tq, tk):
    m = q.shape[0]
    nq = seq_len // tq
    n_tiles = n_batch * N_KV_HEADS * nq
    has_cache = cache is not None

    def tile(u):
        return u // (N_KV_HEADS * nq), (u // nq) % N_KV_HEADS, u % nq

    def score_tile(t):
        return tile(jnp.minimum(t, n_tiles - 1))

    def value_tile(t):
        return tile(jnp.maximum(t - 1, 0))

    def q_map(t):
        b, h, i = score_tile(t)
        return b * nq + i, h

    def k_map(t):
        b, h, _ = score_tile(t)
        return b, h

    def ck_map(t):
        b, h, _ = score_tile(t)
        return b, 0, h

    def vt_map(t):
        b, h, _ = value_tile(t)
        return h, b

    def cv_map(t):
        b, h, _ = value_tile(t)
        return b, 0, h

    def o_map(t):
        b, h, i = value_tile(t)
        return b * nq + i, h

    in_specs = [
        pl.BlockSpec((tq, GROUPS * HEAD_DIM), q_map),
        pl.BlockSpec((seq_len, HEAD_DIM), k_map),
        pl.BlockSpec((HEAD_DIM, seq_len), vt_map),
    ]
    args = [q, k, vt]
    past = 0
    if has_cache:
        past = cache[0].shape[1]
        in_specs += [pl.BlockSpec((None, past, HEAD_DIM), ck_map), pl.BlockSpec((None, past, HEAD_DIM), cv_map)]
        args += list(cache)
    cols = GROUPS * tq
    score_buf = pltpu.VMEM((past + seq_len, cols), F32)
    max_buf = pltpu.VMEM((1, cols), F32)
    return pl.pallas_call(
        functools.partial(_attn_kernel, tq=tq, tk=tk, n_tiles=n_tiles, has_cache=has_cache),
        scratch_shapes=[score_buf, score_buf, max_buf, max_buf],
        grid=(n_tiles + 1,),
        in_specs=in_specs,
        out_specs=pl.BlockSpec((tq, GROUPS * HEAD_DIM), o_map),
        out_shape=jax.ShapeDtypeStruct((m, Q_WIDTH), BF16),
        compiler_params=_params(("arbitrary",), 56),
        name="attn_lat" if has_cache else "attn_ctx",
    )(*args)


def _merge_kernel(x_ref, sh_ref, sc_ref, gate_ref, g_ref, attn_ref, cu_ref, cup_ref, cun_ref, bg_ref, cw_ref,
                  wg_ref, wao_ref, wco_ref, wo_ref, o_ref, h_ref, conv_ref, merged_ref, *, tm, seq_len):

    def conv_rows():
        rows = 32
        w0, w1, w2 = cw_ref[0:1, :], cw_ref[1:2, :], cw_ref[2:3, :]
        zeros = jnp.zeros((HALO, CONV_WIDTH), F32)
        if seq_len > tm:
            tiles_per_seq = seq_len // tm
            tile_in_seq = jnp.full((HALO, CONV_WIDTH), pl.program_id(0) % tiles_per_seq, jnp.int32)
            first_halo = jnp.where(tile_in_seq == 0, 0.0, cup_ref[...])
            last_halo = jnp.where(tile_in_seq == tiles_per_seq - 1, 0.0, cun_ref[...])
        else:
            first_halo = last_halo = zeros
        for r0 in range(0, tm, rows):
            end = r0 + rows
            cur = cu_ref[r0:end, :]
            if r0 == 0:
                prev = first_halo
            elif r0 % seq_len == 0:
                prev = zeros
            else:
                prev = cu_ref[r0 - HALO:r0, :]
            if end == tm:
                nxt = last_halo
            elif end % seq_len == 0:
                nxt = zeros
            else:
                nxt = cu_ref[end:end + HALO, :]
            ext = jnp.concatenate([prev, cur, nxt], axis=0)
            up = pltpu.roll(ext, 1, axis=0)[HALO:HALO + rows]
            dn = pltpu.roll(ext, rows + 2 * HALO - 1, axis=0)[HALO:HALO + rows]
            conv = up * w0 + cur * w1 + dn * w2
            conv_ref[r0:end, :] = (bg_ref[r0:end, :] * conv).astype(BF16)

    _modulated_norm_rows(h_ref, x_ref, g_ref, sc_ref, sh_ref, tm)
    conv_rows()
    tn = 512
    for c0 in range(0, D_MODEL, tn):
        ya = jnp.dot(attn_ref[...], wao_ref[:, c0:c0 + tn], preferred_element_type=F32)
        ga = jax.nn.sigmoid(jnp.dot(h_ref[...], wg_ref[:, c0:c0 + tn], preferred_element_type=F32))
        gs = jax.nn.sigmoid(jnp.dot(h_ref[...], wg_ref[:, D_MODEL + c0:D_MODEL + c0 + tn], preferred_element_type=F32))
        yc = jnp.dot(conv_ref[...], wco_ref[:, c0:c0 + tn], preferred_element_type=F32)
        merged_ref[:, c0:c0 + tn] = (ga * ya + gs * yc).astype(BF16)
    o_ref[...] = x_ref[...] + gate_ref[...] * jnp.dot(merged_ref[...], wo_ref[...], preferred_element_type=F32)


def _merge(x, m3, row_fn, norm_g, attn, cu, bg, conv_w, w_gate, wao, wco, wo, seq_len, tm):
    m = x.shape[0]
    row = lambda i: (i, 0)
    const = lambda i: (0, 0)
    resident = functools.partial(pl.BlockSpec, index_map=const, pipeline_mode=pl.Buffered(1))
    assert seq_len % tm == 0 or tm % seq_len == 0
    blocks_per_tile = tm // HALO
    last_halo = m // HALO - 1
    in_specs = [
        pl.BlockSpec((tm, D_MODEL), row),
        _mod_spec(0, row_fn),
        _mod_spec(1, row_fn),
        _mod_spec(2, row_fn),
        pl.BlockSpec((1, D_MODEL), const),
        pl.BlockSpec((tm, Q_WIDTH), row),
        pl.BlockSpec((tm, CONV_WIDTH), row),
        pl.BlockSpec((HALO, CONV_WIDTH), lambda i: (jnp.maximum(i * blocks_per_tile - 1, 0), 0)),
        pl.BlockSpec((HALO, CONV_WIDTH), lambda i: (jnp.minimum((i + 1) * blocks_per_tile, last_halo), 0)),
        pl.BlockSpec((tm, CONV_WIDTH), row),
        pl.BlockSpec((3, CONV_WIDTH), const),
        resident((D_MODEL, 2 * D_MODEL)),
        resident((Q_WIDTH, D_MODEL)),
        resident((CONV_WIDTH, D_MODEL)),
        resident((D_MODEL, D_MODEL)),
    ]
    return pl.pallas_call(
        functools.partial(_merge_kernel, tm=tm, seq_len=seq_len),
        grid=(m // tm,),
        in_specs=in_specs,
        out_specs=pl.BlockSpec((tm, D_MODEL), row),
        out_shape=jax.ShapeDtypeStruct((m, D_MODEL), F32),
        scratch_shapes=[pltpu.VMEM((tm, D_MODEL), BF16), pltpu.VMEM((tm, CONV_WIDTH), BF16),
                        pltpu.VMEM((tm, D_MODEL), BF16)],
        compiler_params=_params(("parallel",), 56),
        name="merge",
    )(x, m3, m3, m3, norm_g, attn, cu, cu, cu, bg, conv_w, w_gate, wao, wco, wo)


def _mlp_kernel(x_ref, sh_ref, sc_ref, gate_ref, g_ref, fg_ref, w1_ref, w2_ref, o_ref, h_ref):
    fc = pl.program_id(1)
    last = pl.num_programs(1) - 1
    tm = x_ref.shape[0]

    def part():
        f = jnp.maximum(jnp.dot(h_ref[...], w1_ref[...], preferred_element_type=F32), 0.0)
        return jnp.dot((f * f).astype(BF16), w2_ref[...], preferred_element_type=F32)

    @pl.when(fc == 0)
    def _():
        _modulated_norm_rows(h_ref, x_ref, g_ref, sc_ref, sh_ref, tm)
        o_ref[...] = part()

    @pl.when((fc > 0) & (fc < last))
    def _():
        o_ref[...] += part()

    @pl.when(fc == last)
    def _():
        o_ref[...] += part()
        gate = gate_ref[...]
        fg = fg_ref[...]
        rows = 16
        for r0 in range(0, tm, rows):
            y = x_ref[r0:r0 + rows, :] + gate * o_ref[r0:r0 + rows, :]
            ms = jnp.mean(y * y, axis=-1, keepdims=True)
            o_ref[r0:r0 + rows, :] = y * lax.rsqrt(ms + EPS) * fg


def _mlp(x, m3, row_fn, norm_g, final_g, w1, w2, tm, tf):
    m = x.shape[0]
    row = lambda i, j: (i, 0)
    const = lambda i, j: (0, 0)
    in_specs = [
        pl.BlockSpec((tm, D_MODEL), row),
        _mod_spec(3, row_fn),
        _mod_spec(4, row_fn),
        _mod_spec(5, row_fn),
        pl.BlockSpec((1, D_MODEL), const),
        pl.BlockSpec((1, D_MODEL), const),
        pl.BlockSpec((D_MODEL, tf), lambda i, j: (0, j)),
        pl.BlockSpec((tf, D_MODEL), lambda i, j: (j, 0)),
    ]
    return pl.pallas_call(
        _mlp_kernel,
        grid=(m // tm, D_FF // tf),
        in_specs=in_specs,
        out_specs=pl.BlockSpec((tm, D_MODEL), row),
        out_shape=jax.ShapeDtypeStruct((m, D_MODEL), F32),
        scratch_shapes=[pltpu.VMEM((tm, D_MODEL), BF16)],
        compiler_params=_params(("parallel", "arbitrary"), 60),
        name="mlp",
    )(x, m3, m3, m3, norm_g, final_g, w1, w2)


def _rope_tables(n_tokens):
    rows = n_tokens // GRID_W
    row = jnp.broadcast_to(jnp.arange(rows)[:, None], (rows, GRID_W)).reshape(-1).astype(F32)
    col = jnp.broadcast_to(jnp.arange(GRID_W)[None, :], (rows, GRID_W)).reshape(-1).astype(F32)
    half = HEAD_DIM // 2
    inv = ROPE_THETA ** (-jnp.arange(0, half, 2, dtype=F32) / half)
    ang = jnp.concatenate([row[:, None] * inv, col[:, None] * inv], axis=-1)
    cos2 = jnp.repeat(jnp.cos(ang), 2, axis=-1)
    sign = jnp.tile(jnp.array([-1.0, 1.0], F32), half)
    sin2 = jnp.repeat(jnp.sin(ang), 2, axis=-1) * sign
    return cos2, sin2


def _stream(x, m3, row_fn, seq_len, rope_tabs, cache, emit_kv, lw, tiles):
    n_batch = x.shape[0]
    xf = x.reshape(n_batch * seq_len, D_MODEL)
    proj = _inproj(xf, m3, functools.partial(row_fn, tm=tiles["inproj"]), lw["norm1_g"], lw["w_proj"], lw["q_gain"],
                   lw["k_gain"], rope_tabs, seq_len, emit_kv, tiles["inproj"])
    if emit_kv:
        q, k, vt, kf, vf, cu, bg = proj
    else:
        q, k, vt, cu, bg = proj
        kf = vf = None
    attn = _attention(q, k, vt, cache, n_batch, seq_len, tiles["tq"], tiles["tk"])
    x1 = _merge(xf, m3, functools.partial(row_fn, tm=tiles["merge"]), lw["norm1_g"], attn, cu, bg, lw["conv_w"],
                lw["w_gate"], lw["w_attn_out"], lw["w_conv_out"], lw["w_o"], seq_len, tiles["merge"])
    y = _mlp(x1, m3, functools.partial(row_fn, tm=tiles["mlp"]), lw["norm2_g"], lw["final_g"], lw["w_ff1"],
             lw["w_ff2"], tiles["mlp"], tiles["tf"])
    return y.reshape(x.shape), kf, vf


def _ctx_row(i, tm):
    return 0


def _lat_row(i, tm, seq_len):
    return 1 + (i * tm) // seq_len


def kernel(x_prompt, x_sample, cache_k, cache_v, c, c_ctx, w_mod, b_mod, norm1_g, norm2_g, w_in, q_gain, k_gain,
           conv_w, w_attn_out, w_conv_out, w_o, w_ff1, w_ff2, final_g):
    depth = w_mod.shape[0]
    batch, seq, _ = x_prompt.shape
    dec_batch, dec_seq, _ = x_sample.shape
    past = cache_k.shape[2]
    assert depth == 1 and 1 + dec_batch <= COND_ROWS

    cond = jnp.zeros((COND_ROWS, D_MODEL), F32).at[0].set(c_ctx).at[1:1 + dec_batch].set(c)
    rope_tabs = _rope_tables(dec_seq)

    xp, xs = x_prompt, x_sample
    new_ks, new_vs = [], []
    for l in range(depth):
        lw = {
            "norm1_g": norm1_g[l].reshape(1, D_MODEL),
            "norm2_g": norm2_g[l].reshape(1, D_MODEL),
            "final_g": final_g.reshape(1, D_MODEL),
            "q_gain": q_gain[l].reshape(1, HEAD_DIM),
            "k_gain": k_gain[l].reshape(1, HEAD_DIM),
            "conv_w": conv_w[l],
            "w_proj": w_in[l, :, :PROJ_WIDTH].astype(BF16),
            "w_gate": w_in[l, :, PROJ_WIDTH:].astype(BF16),
            "w_attn_out": w_attn_out[l].astype(BF16),
            "w_conv_out": w_conv_out[l].astype(BF16),
            "w_o": w_o[l].astype(BF16),
            "w_ff1": w_ff1[l].astype(BF16),
            "w_ff2": w_ff2[l].astype(BF16),
        }
        m3 = _modulation(cond, w_mod[l], b_mod[l]).reshape(COND_ROWS, 1, N_MOD * D_MODEL)
        cache = (cache_k[:, l].reshape(dec_batch, past, KV_WIDTH), cache_v[:, l].reshape(dec_batch, past, KV_WIDTH))
        ctx_tiles = {"inproj": 512, "tq": seq, "tk": seq, "merge": 256, "mlp": 512, "tf": 2048}
        lat_tiles = {"inproj": 512, "tq": 256, "tk": 256, "merge": 256, "mlp": 512, "tf": 2048}
        xp, kf, vf = _stream(xp, m3, _ctx_row, seq, None, None, True, lw, ctx_tiles)
        xs, _, _ = _stream(xs, m3, functools.partial(_lat_row, seq_len=dec_seq), dec_seq, rope_tabs, cache, False,
                           lw, lat_tiles)
        new_ks.append(kf.reshape(batch, seq, N_KV_HEADS, HEAD_DIM))
        new_vs.append(vf.reshape(batch, seq, N_KV_HEADS, HEAD_DIM))
    return xp, xs, jnp.stack(new_ks, axis=1), jnp.stack(new_vs, axis=1)
```

```python
import functools

import jax
import jax.numpy as jnp
from jax import lax
from jax.experimental import pallas as pl
from jax.experimental.pallas import tpu as pltpu

D_MODEL = 2048
GRID_W = 64
HEAD_DIM = 128
N_HEADS = 8
N_KV_HEADS = 2
GROUPS = N_HEADS // N_KV_HEADS
Q_WIDTH = N_HEADS * HEAD_DIM
KV_WIDTH = N_KV_HEADS * HEAD_DIM
CONV_WIDTH = D_MODEL // 2
D_FF = 4 * D_MODEL
ROPE_THETA = 10000.0
EPS = 1e-6
N_MOD = 6
LOG2_E = 1.4426950408889634
ATTN_COL_GROUP = 1024
COND_ROWS = 16
HALO = 8

OFF_Q = 0
OFF_K = Q_WIDTH
OFF_V = OFF_K + KV_WIDTH
OFF_U = OFF_V + KV_WIDTH
OFF_C = OFF_U + CONV_WIDTH
OFF_B = OFF_C + CONV_WIDTH
OFF_A = OFF_B + CONV_WIDTH
OFF_S = OFF_A + D_MODEL
IN_WIDTH = OFF_S + D_MODEL
PROJ_WIDTH = OFF_A

MIB = 1024 * 1024
F32 = jnp.float32
BF16 = jnp.bfloat16


def _params(sem, vmem_mib):
    return pltpu.CompilerParams(dimension_semantics=sem, vmem_limit_bytes=vmem_mib * MIB)


def _mod_spec(chunk, row_fn):
    return pl.BlockSpec((None, 1, D_MODEL), lambda i, *_: (row_fn(i), 0, chunk))


def _modulated_norm(x, g, sc, sh):
    ms = jnp.mean(x * x, axis=-1, keepdims=True)
    y = x * lax.rsqrt(ms + EPS) * g
    return y * (1.0 + sc) + sh


def _modulated_norm_rows(h_ref, x_ref, g_ref, sc_ref, sh_ref, tm):
    rows = 16
    gs = g_ref[...] * (1.0 + sc_ref[...])
    sh = sh_ref[...]
    for r0 in range(0, tm, rows):
        x = x_ref[r0:r0 + rows, :]
        ms = jnp.mean(x * x, axis=-1, keepdims=True)
        h_ref[r0:r0 + rows, :] = (x * lax.rsqrt(ms + EPS) * gs + sh).astype(BF16)


def _head_norm(z, gain):
    ms = jnp.mean(z * z, axis=-1, keepdims=True)
    return z * lax.rsqrt(ms + EPS) * gain


def _rope(x, cos2, sin2):
    lane = lax.broadcasted_iota(jnp.int32, x.shape, 1)
    nxt = pltpu.roll(x, HEAD_DIM - 1, axis=1)
    prv = pltpu.roll(x, 1, axis=1)
    swapped = jnp.where(lane % 2 == 0, nxt, prv)
    return x * cos2 + swapped * sin2


def _mod_kernel(cond_ref, w_ref, b_ref, o_ref):
    cond = cond_ref[...]
    a = (cond * jax.nn.sigmoid(cond)).astype(BF16)
    o_ref[...] = jnp.dot(a, w_ref[...].astype(BF16), preferred_element_type=F32) + b_ref[...]


def _modulation(cond, w_mod, b_mod):
    tn = 1024
    n = w_mod.shape[1]
    return pl.pallas_call(
        _mod_kernel,
        grid=(n // tn,),
        in_specs=[
            pl.BlockSpec((COND_ROWS, D_MODEL), lambda j: (0, 0)),
            pl.BlockSpec((D_MODEL, tn), lambda j: (0, j)),
            pl.BlockSpec((1, tn), lambda j: (0, j)),
        ],
        out_specs=pl.BlockSpec((COND_ROWS, tn), lambda j: (0, j)),
        out_shape=jax.ShapeDtypeStruct((COND_ROWS, n), F32),
        compiler_params=_params(("arbitrary",), 40),
        name="modulation",
    )(cond, w_mod, b_mod.reshape(1, n))


def _inproj_kernel(*refs, rope, emit_kv):
    x_ref, sh_ref, sc_ref, g_ref, w_ref, qg_ref, kg_ref = refs[:7]
    pos = 7
    if rope:
        cos_ref, sin_ref = refs[pos:pos + 2]
        pos += 2
    q_ref, k_ref, vt_ref = refs[pos:pos + 3]
    pos += 3
    if emit_kv:
        kf_ref, vf_ref = refs[pos:pos + 2]
        pos += 2
    cu_ref, bg_ref, h_ref = refs[pos:pos + 3]

    _modulated_norm_rows(h_ref, x_ref, g_ref, sc_ref, sh_ref, x_ref.shape[0])

    def proj(off, width):
        return jnp.dot(h_ref[...], w_ref[:, off:off + width], preferred_element_type=F32)

    if rope:
        cos2 = cos_ref[...]
        sin2 = sin_ref[...]
    scale = HEAD_DIM ** -0.5 * LOG2_E

    for hd in range(N_HEADS // 4):
        z = proj(OFF_Q + hd * 4 * HEAD_DIM, 4 * HEAD_DIM)
        for hh in range(4):
            qn = _head_norm(z[:, hh * HEAD_DIM:(hh + 1) * HEAD_DIM], qg_ref[...])
            if rope:
                qn = _rope(qn, cos2, sin2)
            col = (hd * 4 + hh) * HEAD_DIM
            q_ref[:, col:col + HEAD_DIM] = (qn * scale).astype(BF16)

    z = proj(OFF_K, 2 * KV_WIDTH)
    for hh in range(N_KV_HEADS):
        col = hh * HEAD_DIM
        kn = _head_norm(z[:, col:col + HEAD_DIM], kg_ref[...])
        if emit_kv:
            kf_ref[:, col:col + HEAD_DIM] = kn
        if rope:
            kn = _rope(kn, cos2, sin2)
        k_ref[:, col:col + HEAD_DIM] = kn.astype(BF16)
    vv = z[:, KV_WIDTH:]
    if emit_kv:
        vf_ref[...] = vv
    vt_ref[...] = vv.T.astype(BF16)

    cw = 256
    for c in range(CONV_WIDTH // cw):
        u = proj(OFF_U + c * cw, cw)
        gc = proj(OFF_C + c * cw, cw)
        cu_ref[:, c * cw:(c + 1) * cw] = gc * u

    bw = 512
    for c in range(CONV_WIDTH // bw):
        bg_ref[:, c * bw:(c + 1) * bw] = proj(OFF_B + c * bw, bw)


def _inproj(x, m3, row_fn, norm_g, w_in_bf, q_gain, k_gain, rope_tabs, seq_len, emit_kv, tm):
    m = x.shape[0]
    rope = rope_tabs is not None
    row = lambda i: (i, 0)
    const = lambda i: (0, 0)
    in_specs = [
        pl.BlockSpec((tm, D_MODEL), row),
        _mod_spec(0, row_fn),
        _mod_spec(1, row_fn),
        pl.BlockSpec((1, D_MODEL), const),
        pl.BlockSpec((D_MODEL, PROJ_WIDTH), const, pipeline_mode=pl.Buffered(1)),
        pl.BlockSpec((1, HEAD_DIM), const),
        pl.BlockSpec((1, HEAD_DIM), const),
    ]
    args = [x, m3, m3, norm_g, w_in_bf, q_gain, k_gain]
    if rope:
        tiles_per_seq = seq_len // tm
        tab_spec = pl.BlockSpec((tm, HEAD_DIM), lambda i: (i % tiles_per_seq, 0))
        in_specs += [tab_spec, tab_spec]
        args += list(rope_tabs)
    out_specs = [
        pl.BlockSpec((tm, Q_WIDTH), row),
        pl.BlockSpec((tm, KV_WIDTH), row),
        pl.BlockSpec((KV_WIDTH, tm), lambda i: (0, i)),
    ]
    out_shape = [
        jax.ShapeDtypeStruct((m, Q_WIDTH), BF16),
        jax.ShapeDtypeStruct((m, KV_WIDTH), BF16),
        jax.ShapeDtypeStruct((KV_WIDTH, m), BF16),
    ]
    if emit_kv:
        out_specs += [pl.BlockSpec((tm, KV_WIDTH), row)] * 2
        out_shape += [jax.ShapeDtypeStruct((m, KV_WIDTH), F32)] * 2
    out_specs += [pl.BlockSpec((tm, CONV_WIDTH), row)] * 2
    out_shape += [jax.ShapeDtypeStruct((m, CONV_WIDTH), F32)] * 2
    return pl.pallas_call(
        functools.partial(_inproj_kernel, rope=rope, emit_kv=emit_kv),
        grid=(m // tm,),
        in_specs=in_specs,
        out_specs=out_specs,
        out_shape=out_shape,
        scratch_shapes=[pltpu.VMEM((tm, D_MODEL), BF16)],
        compiler_params=_params(("parallel",), 56),
        name="inproj_lat" if rope else "inproj_ctx",
    )(*args)


def _attn_kernel(*refs, tq, tk, n_tiles, has_cache):
    if has_cache:
        q_ref, k_ref, vt_ref, ck_ref, cv_ref, o_ref, s_ref, m_ref = refs
        past = ck_ref.shape[0]
    else:
        q_ref, k_ref, vt_ref, o_ref, s_ref, m_ref = refs
        past = 0
    t = pl.program_id(0)
    nt = (((1,), (1,)), ((), ()))

    seq_len = k_ref.shape[0]
    chunks = ([(0, past)] if has_cache else []) + [(past + c, tk) for c in range(0, seq_len, tk)]

    def step(do_score, do_value):
        cols = GROUPS * tq
        cw = min(cols, ATTN_COL_GROUP)
        groups = range(cols // cw)
        if do_score:
            qs = jnp.concatenate([q_ref[:, g * HEAD_DIM:(g + 1) * HEAD_DIM] for g in range(GROUPS)], axis=0)
        if do_value:
            m_old = [m_ref[:, j * cw:(j + 1) * cw] for j in groups]
        m_new = [None for _ in groups]
        l_sum = [None for _ in groups]
        acc = [None for _ in groups]
        for off, n in chunks:
            cached = off < past
            if do_value:
                vtc = cv_ref[...].T.astype(BF16) if cached else vt_ref[:, off - past:off - past + n]
            if do_score:
                kc = ck_ref[...].astype(BF16) if cached else k_ref[off - past:off - past + n, :]
            for j in groups:
                c0 = j * cw
                if do_value:
                    p = jnp.exp2(s_ref[off:off + n, c0:c0 + cw] - m_old[j])
                    ps = jnp.sum(p, axis=0, keepdims=True)
                    l_sum[j] = ps if l_sum[j] is None else l_sum[j] + ps
                    pv = jnp.dot(vtc, p.astype(BF16), preferred_element_type=F32)
                    acc[j] = pv if acc[j] is None else acc[j] + pv
                if do_score:
                    st = lax.dot_general(kc, qs[c0:c0 + cw, :], nt, preferred_element_type=F32)
                    s_ref[off:off + n, c0:c0 + cw] = st
                    cm = jnp.max(st, axis=0, keepdims=True)
                    m_new[j] = cm if m_new[j] is None else jnp.maximum(m_new[j], cm)
        for j in groups:
            c0 = j * cw
            if do_score:
                m_ref[:, c0:c0 + cw] = m_new[j]
            if do_value:
                out_t = acc[j] / l_sum[j]
                for g in range(c0 // tq, (c0 + cw) // tq):
                    lo = g * tq - c0
                    o_ref[:, g * HEAD_DIM:(g + 1) * HEAD_DIM] = out_t[:, lo:lo + tq].T.astype(BF16)

    @pl.when(t == 0)
    def _():
        step(True, False)

    @pl.when((t > 0) & (t < n_tiles))
    def _():
        step(True, True)

    @pl.when(t == n_tiles)
    def _():
        step(False, True)


def _attention(q, k, vt, cache, n_batch, seq_len, tq, tk):
    m = q.shape[0]
    nq = seq_len // tq
    n_tiles = n_batch * N_KV_HEADS * nq
    has_cache = cache is not None

    def tile(u):
        return u // (N_KV_HEADS * nq), (u // nq) % N_KV_HEADS, u % nq

    def score_tile(t):
        return tile(jnp.minimum(t, n_tiles - 1))

    def value_tile(t):
        return tile(jnp.maximum(t - 1, 0))

    def q_map(t):
        b, h, i = score_tile(t)
        return b * nq + i, h

    def k_map(t):
        b, h, _ = score_tile(t)
        return b, h

    def ck_map(t):
        b, h, _ = score_tile(t)
        return b, 0, h

    def vt_map(t):
        b, h, _ = value_tile(t)
        return h, b

    def cv_map(t):
        b, h, _ = value_tile(t)
        return b, 0, h

    def o_map(t):
        b, h, i = value_tile(t)
        return b * nq + i, h

    in_specs = [
        pl.BlockSpec((tq, GROUPS * HEAD_DIM), q_map),
        pl.BlockSpec((seq_len, HEAD_DIM), k_map),
        pl.BlockSpec((HEAD_DIM, seq_len), vt_map),
    ]
    args = [q, k, vt]
    past = 0
    if has_cache:
        past = cache[0].shape[1]
        in_specs += [pl.BlockSpec((None, past, HEAD_DIM), ck_map), pl.BlockSpec((None, past, HEAD_DIM), cv_map)]
        args += list(cache)
    cols = GROUPS * tq
    return pl.pallas_call(
        functools.partial(_attn_kernel, tq=tq, tk=tk, n_tiles=n_tiles, has_cache=has_cache),
        scratch_shapes=[pltpu.VMEM((past + seq_len, cols), F32), pltpu.VMEM((1, cols), F32)],
        grid=(n_tiles + 1,),
        in_specs=in_specs,
        out_specs=pl.BlockSpec((tq, GROUPS * HEAD_DIM), o_map),
        out_shape=jax.ShapeDtypeStruct((m, Q_WIDTH), BF16),
        compiler_params=_params(("arbitrary",), 56),
        name="attn_lat" if has_cache else "attn_ctx",
    )(*args)


def _merge_kernel(x_ref, sh_ref, sc_ref, gate_ref, g_ref, attn_ref, cu_ref, cup_ref, cun_ref, bg_ref, cw_ref,
                  wg_ref, wao_ref, wco_ref, wo_ref, o_ref, h_ref, conv_ref, merged_ref, *, tm, seq_len):

    def conv_rows():
        rows = 32
        w0, w1, w2 = cw_ref[0:1, :], cw_ref[1:2, :], cw_ref[2:3, :]
        zeros = jnp.zeros((HALO, CONV_WIDTH), F32)
        if seq_len > tm:
            tiles_per_seq = seq_len // tm
            tile_in_seq = jnp.full((HALO, CONV_WIDTH), pl.program_id(0) % tiles_per_seq, jnp.int32)
            first_halo = jnp.where(tile_in_seq == 0, 0.0, cup_ref[...])
            last_halo = jnp.where(tile_in_seq == tiles_per_seq - 1, 0.0, cun_ref[...])
        else:
            first_halo = last_halo = zeros
        for r0 in range(0, tm, rows):
            end = r0 + rows
            cur = cu_ref[r0:end, :]
            if r0 == 0:
                prev = first_halo
            elif r0 % seq_len == 0:
                prev = zeros
            else:
                prev = cu_ref[r0 - HALO:r0, :]
            if end == tm:
                nxt = last_halo
            elif end % seq_len == 0:
                nxt = zeros
            else:
                nxt = cu_ref[end:end + HALO, :]
            ext = jnp.concatenate([prev, cur, nxt], axis=0)
            up = pltpu.roll(ext, 1, axis=0)[HALO:HALO + rows]
            dn = pltpu.roll(ext, rows + 2 * HALO - 1, axis=0)[HALO:HALO + rows]
            conv = up * w0 + cur * w1 + dn * w2
            conv_ref[r0:end, :] = (bg_ref[r0:end, :] * conv).astype(BF16)

    _modulated_norm_rows(h_ref, x_ref, g_ref, sc_ref, sh_ref, tm)
    conv_rows()
    tn = 512
    for c0 in range(0, D_MODEL, tn):
        ya = jnp.dot(attn_ref[...], wao_ref[:, c0:c0 + tn], preferred_element_type=F32)
        ga = jax.nn.sigmoid(jnp.dot(h_ref[...], wg_ref[:, c0:c0 + tn], preferred_element_type=F32))
        gs = jax.nn.sigmoid(jnp.dot(h_ref[...], wg_ref[:, D_MODEL + c0:D_MODEL + c0 + tn], preferred_element_type=F32))
        yc = jnp.dot(conv_ref[...], wco_ref[:, c0:c0 + tn], preferred_element_type=F32)
        merged_ref[:, c0:c0 + tn] = (ga * ya + gs * yc).astype(BF16)
    o_ref[...] = x_ref[...] + gate_ref[...] * jnp.dot(merged_ref[...], wo_ref[...], preferred_element_type=F32)


def _merge(x, m3, row_fn, norm_g, attn, cu, bg, conv_w, w_gate, wao, wco, wo, seq_len, tm):
    m = x.shape[0]
    row = lambda i: (i, 0)
    const = lambda i: (0, 0)
    resident = functools.partial(pl.BlockSpec, index_map=const, pipeline_mode=pl.Buffered(1))
    assert seq_len % tm == 0 or tm % seq_len == 0
    blocks_per_tile = tm // HALO
    last_halo = m // HALO - 1
    in_specs = [
        pl.BlockSpec((tm, D_MODEL), row),
        _mod_spec(0, row_fn),
        _mod_spec(1, row_fn),
        _mod_spec(2, row_fn),
        pl.BlockSpec((1, D_MODEL), const),
        pl.BlockSpec((tm, Q_WIDTH), row),
        pl.BlockSpec((tm, CONV_WIDTH), row),
        pl.BlockSpec((HALO, CONV_WIDTH), lambda i: (jnp.maximum(i * blocks_per_tile - 1, 0), 0)),
        pl.BlockSpec((HALO, CONV_WIDTH), lambda i: (jnp.minimum((i + 1) * blocks_per_tile, last_halo), 0)),
        pl.BlockSpec((tm, CONV_WIDTH), row),
        pl.BlockSpec((3, CONV_WIDTH), const),
        resident((D_MODEL, 2 * D_MODEL)),
        resident((Q_WIDTH, D_MODEL)),
        resident((CONV_WIDTH, D_MODEL)),
        resident((D_MODEL, D_MODEL)),
    ]
    return pl.pallas_call(
        functools.partial(_merge_kernel, tm=tm, seq_len=seq_len),
        grid=(m // tm,),
        in_specs=in_specs,
        out_specs=pl.BlockSpec((tm, D_MODEL), row),
        out_shape=jax.ShapeDtypeStruct((m, D_MODEL), F32),
        scratch_shapes=[pltpu.VMEM((tm, D_MODEL), BF16), pltpu.VMEM((tm, CONV_WIDTH), BF16),
                        pltpu.VMEM((tm, D_MODEL), BF16)],
        compiler_params=_params(("parallel",), 56),
        name="merge",
    )(x, m3, m3, m3, norm_g, attn, cu, cu, cu, bg, conv_w, w_gate, wao, wco, wo)


def _mlp_kernel(x_ref, sh_ref, sc_ref, gate_ref, g_ref, fg_ref, w1_ref, w2_ref, o_ref, h_ref):
    fc = pl.program_id(1)
    last = pl.num_programs(1) - 1
    tm = x_ref.shape[0]

    def part():
        f = jnp.maximum(jnp.dot(h_ref[...], w1_ref[...], preferred_element_type=F32), 0.0)
        return jnp.dot((f * f).astype(BF16), w2_ref[...], preferred_element_type=F32)

    @pl.when(fc == 0)
    def _():
        _modulated_norm_rows(h_ref, x_ref, g_ref, sc_ref, sh_ref, tm)
        o_ref[...] = part()

    @pl.when((fc > 0) & (fc < last))
    def _():
        o_ref[...] += part()

    @pl.when(fc == last)
    def _():
        o_ref[...] += part()
        gate = gate_ref[...]
        fg = fg_ref[...]
        rows = 16
        for r0 in range(0, tm, rows):
            y = x_ref[r0:r0 + rows, :] + gate * o_ref[r0:r0 + rows, :]
            ms = jnp.mean(y * y, axis=-1, keepdims=True)
            o_ref[r0:r0 + rows, :] = y * lax.rsqrt(ms + EPS) * fg


def _mlp(x, m3, row_fn, norm_g, final_g, w1, w2, tm, tf):
    m = x.shape[0]
    row = lambda i, j: (i, 0)
    const = lambda i, j: (0, 0)
    in_specs = [
        pl.BlockSpec((tm, D_MODEL), row),
        _mod_spec(3, row_fn),
        _mod_spec(4, row_fn),
        _mod_spec(5, row_fn),
        pl.BlockSpec((1, D_MODEL), const),
        pl.BlockSpec((1, D_MODEL), const),
        pl.BlockSpec((D_MODEL, tf), lambda i, j: (0, j)),
        pl.BlockSpec((tf, D_MODEL), lambda i, j: (j, 0)),
    ]
    return pl.pallas_call(
        _mlp_kernel,
        grid=(m // tm, D_FF // tf),
        in_specs=in_specs,
        out_specs=pl.BlockSpec((tm, D_MODEL), row),
        out_shape=jax.ShapeDtypeStruct((m, D_MODEL), F32),
        scratch_shapes=[pltpu.VMEM((tm, D_MODEL), BF16)],
        compiler_params=_params(("parallel", "arbitrary"), 60),
        name="mlp",
    )(x, m3, m3, m3, norm_g, final_g, w1, w2)


def _rope_tables(n_tokens):
    rows = n_tokens // GRID_W
    row = jnp.broadcast_to(jnp.arange(rows)[:, None], (rows, GRID_W)).reshape(-1).astype(F32)
    col = jnp.broadcast_to(jnp.arange(GRID_W)[None, :], (rows, GRID_W)).reshape(-1).astype(F32)
    half = HEAD_DIM // 2
    inv = ROPE_THETA ** (-jnp.arange(0, half, 2, dtype=F32) / half)
    ang = jnp.concatenate([row[:, None] * inv, col[:, None] * inv], axis=-1)
    cos2 = jnp.repeat(jnp.cos(ang), 2, axis=-1)
    sign = jnp.tile(jnp.array([-1.0, 1.0], F32), half)
    sin2 = jnp.repeat(jnp.sin(ang), 2, axis=-1) * sign
    return cos2, sin2


def _stream(x, m3, row_fn, seq_len, rope_tabs, cache, emit_kv, lw, tiles):
    n_batch = x.shape[0]
    xf = x.reshape(n_batch * seq_len, D_MODEL)
    proj = _inproj(xf, m3, functools.partial(row_fn, tm=tiles["inproj"]), lw["norm1_g"], lw["w_proj"], lw["q_gain"],
                   lw["k_gain"], rope_tabs, seq_len, emit_kv, tiles["inproj"])
    if emit_kv:
        q, k, vt, kf, vf, cu, bg = proj
    else:
        q, k, vt, cu, bg = proj
        kf = vf = None
    attn = _attention(q, k, vt, cache, n_batch, seq_len, tiles["tq"], tiles["tk"])
    x1 = _merge(xf, m3, functools.partial(row_fn, tm=tiles["merge"]), lw["norm1_g"], attn, cu, bg, lw["conv_w"],
                lw["w_gate"], lw["w_attn_out"], lw["w_conv_out"], lw["w_o"], seq_len, tiles["merge"])
    y = _mlp(x1, m3, functools.partial(row_fn, tm=tiles["mlp"]), lw["norm2_g"], lw["final_g"], lw["w_ff1"],
             lw["w_ff2"], tiles["mlp"], tiles["tf"])
    return y.reshape(x.shape), kf, vf


def _ctx_row(i, tm):
    return 0


def _lat_row(i, tm, seq_len):
    return 1 + (i * tm) // seq_len


def kernel(x_prompt, x_sample, cache_k, cache_v, c, c_ctx, w_mod, b_mod, norm1_g, norm2_g, w_in, q_gain, k_gain,
           conv_w, w_attn_out, w_conv_out, w_o, w_ff1, w_ff2, final_g):
    depth = w_mod.shape[0]
    batch, seq, _ = x_prompt.shape
    dec_batch, dec_seq, _ = x_sample.shape
    past = cache_k.shape[2]
    assert depth == 1 and 1 + dec_batch <= COND_ROWS

    cond = jnp.zeros((COND_ROWS, D_MODEL), F32).at[0].set(c_ctx).at[1:1 + dec_batch].set(c)
    rope_tabs = _rope_tables(dec_seq)

    xp, xs = x_prompt, x_sample
    new_ks, new_vs = [], []
    for l in range(depth):
        lw = {
            "norm1_g": norm1_g[l].reshape(1, D_MODEL),
            "norm2_g": norm2_g[l].reshape(1, D_MODEL),
            "final_g": final_g.reshape(1, D_MODEL),
            "q_gain": q_gain[l].reshape(1, HEAD_DIM),
            "k_gain": k_gain[l].reshape(1, HEAD_DIM),
            "conv_w": conv_w[l],
            "w_proj": w_in[l, :, :PROJ_WIDTH].astype(BF16),
            "w_gate": w_in[l, :, PROJ_WIDTH:].astype(BF16),
            "w_attn_out": w_attn_out[l].astype(BF16),
            "w_conv_out": w_conv_out[l].astype(BF16),
            "w_o": w_o[l].astype(BF16),
            "w_ff1": w_ff1[l].astype(BF16),
            "w_ff2": w_ff2[l].astype(BF16),
        }
        m3 = _modulation(cond, w_mod[l], b_mod[l]).reshape(COND_ROWS, 1, N_MOD * D_MODEL)
        cache = (cache_k[:, l].reshape(dec_batch, past, KV_WIDTH), cache_v[:, l].reshape(dec_batch, past, KV_WIDTH))
        ctx_tiles = {"inproj": 512, "tq": seq, "tk": seq, "merge": 256, "mlp": 512, "tf": 2048}
        lat_tiles = {"inproj": 512, "tq": 512, "tk": 256, "merge": 256, "mlp": 512, "tf": 2048}
        xp, kf, vf = _stream(xp, m3, _ctx_row, seq, None, None, True, lw, ctx_tiles)
        xs, _, _ = _stream(xs, m3, functools.partial(_lat_row, seq_len=dec_seq), dec_seq, rope_tabs, cache, False,
                           lw, lat_tiles)
        new_ks.append(kf.reshape(batch, seq, N_KV_HEADS, HEAD_DIM))
        new_vs.append(vf.reshape(batch, seq, N_KV_HEADS, HEAD_DIM))
    return xp, xs, jnp.stack(new_ks, axis=1), jnp.stack(new_vs, axis=1)
```

```python
import functools

import jax
import jax.numpy as jnp
from jax import lax
from jax.experimental import pallas as pl
from jax.experimental.pallas import tpu as pltpu

D_MODEL = 2048
GRID_W = 64
HEAD_DIM = 128
N_HEADS = 8
N_KV_HEADS = 2
GROUPS = N_HEADS // N_KV_HEADS
Q_WIDTH = N_HEADS * HEAD_DIM
KV_WIDTH = N_KV_HEADS * HEAD_DIM
CONV_WIDTH = D_MODEL // 2
D_FF = 4 * D_MODEL
ROPE_THETA = 10000.0
EPS = 1e-6
N_MOD = 6
LOG2_E = 1.4426950408889634
ATTN_COL_GROUP = 1024
COND_ROWS = 16
BF16_SUBLANES = 16
HALO = 8

OFF_Q = 0
OFF_K = Q_WIDTH
OFF_V = OFF_K + KV_WIDTH
OFF_U = OFF_V + KV_WIDTH
OFF_C = OFF_U + CONV_WIDTH
OFF_B = OFF_C + CONV_WIDTH
OFF_A = OFF_B + CONV_WIDTH
OFF_S = OFF_A + D_MODEL
IN_WIDTH = OFF_S + D_MODEL
PROJ_WIDTH = OFF_A

MIB = 1024 * 1024
F32 = jnp.float32
BF16 = jnp.bfloat16


def _params(sem, vmem_mib):
    return pltpu.CompilerParams(dimension_semantics=sem, vmem_limit_bytes=vmem_mib * MIB)


def _mod_spec(chunk, row_fn):
    return pl.BlockSpec((None, 1, D_MODEL), lambda i, *_: (row_fn(i), 0, chunk))


def _modulated_norm(x, g, sc, sh):
    ms = jnp.mean(x * x, axis=-1, keepdims=True)
    y = x * lax.rsqrt(ms + EPS) * g
    return y * (1.0 + sc) + sh


def _modulated_norm_rows(h_ref, x_ref, g_ref, sc_ref, sh_ref, tm):
    rows = 16
    gs = g_ref[...] * (1.0 + sc_ref[...])
    sh = sh_ref[...]
    for r0 in range(0, tm, rows):
        x = x_ref[r0:r0 + rows, :]
        ms = jnp.mean(x * x, axis=-1, keepdims=True)
        h_ref[r0:r0 + rows, :] = (x * lax.rsqrt(ms + EPS) * gs + sh).astype(BF16)


def _head_norm(z, gain):
    ms = jnp.mean(z * z, axis=-1, keepdims=True)
    return z * lax.rsqrt(ms + EPS) * gain


def _rope(x, cos2, sin2):
    lane = lax.broadcasted_iota(jnp.int32, x.shape, 1)
    nxt = pltpu.roll(x, HEAD_DIM - 1, axis=1)
    prv = pltpu.roll(x, 1, axis=1)
    swapped = jnp.where(lane % 2 == 0, nxt, prv)
    return x * cos2 + swapped * sin2


def _mod_kernel(cond_ref, w_ref, b_ref, o_ref):
    cond = cond_ref[...]
    a = (cond * jax.nn.sigmoid(cond)).astype(BF16)
    o_ref[...] = jnp.dot(a, w_ref[...].astype(BF16), preferred_element_type=F32) + b_ref[...]


def _modulation(cond, w_mod, b_mod):
    tn = 1024
    n = w_mod.shape[1]
    return pl.pallas_call(
        _mod_kernel,
        grid=(n // tn,),
        in_specs=[
            pl.BlockSpec((COND_ROWS, D_MODEL), lambda j: (0, 0)),
            pl.BlockSpec((D_MODEL, tn), lambda j: (0, j)),
            pl.BlockSpec((1, tn), lambda j: (0, j)),
        ],
        out_specs=pl.BlockSpec((COND_ROWS, tn), lambda j: (0, j)),
        out_shape=jax.ShapeDtypeStruct((COND_ROWS, n), F32),
        compiler_params=_params(("arbitrary",), 40),
        name="modulation",
    )(cond, w_mod, b_mod.reshape(1, n))


def _inproj_kernel(*refs, rope, emit_kv, n_cast):
    x_ref, sh_ref, sc_ref, g_ref, w_ref, qg_ref, kg_ref = refs[:7]
    pos = 7
    if rope:
        cos_ref, sin_ref = refs[pos:pos + 2]
        pos += 2
    cast_in = refs[pos:pos + n_cast]
    pos += n_cast
    q_ref, k_ref, vt_ref = refs[pos:pos + 3]
    pos += 3
    if emit_kv:
        kf_ref, vf_ref = refs[pos:pos + 2]
        pos += 2
    cu_ref, bg_ref = refs[pos:pos + 2]
    pos += 2
    cast_out = refs[pos:pos + n_cast]
    h_ref = refs[pos + n_cast]

    for src, dst in zip(cast_in, cast_out):
        dst[...] = src[:, src.shape[1] - dst.shape[1]:].astype(BF16)

    _modulated_norm_rows(h_ref, x_ref, g_ref, sc_ref, sh_ref, x_ref.shape[0])

    def proj(off, width):
        return jnp.dot(h_ref[...], w_ref[:, off:off + width], preferred_element_type=F32)

    if rope:
        cos2 = cos_ref[...]
        sin2 = sin_ref[...]
    scale = HEAD_DIM ** -0.5 * LOG2_E

    for hd in range(N_HEADS // 4):
        z = proj(OFF_Q + hd * 4 * HEAD_DIM, 4 * HEAD_DIM)
        for hh in range(4):
            qn = _head_norm(z[:, hh * HEAD_DIM:(hh + 1) * HEAD_DIM], qg_ref[...])
            if rope:
                qn = _rope(qn, cos2, sin2)
            col = (hd * 4 + hh) * HEAD_DIM
            q_ref[:, col:col + HEAD_DIM] = (qn * scale).astype(BF16)

    z = proj(OFF_K, 2 * KV_WIDTH)
    for hh in range(N_KV_HEADS):
        col = hh * HEAD_DIM
        kn = _head_norm(z[:, col:col + HEAD_DIM], kg_ref[...])
        if emit_kv:
            kf_ref[:, col:col + HEAD_DIM] = kn
        if rope:
            kn = _rope(kn, cos2, sin2)
        k_ref[:, col:col + HEAD_DIM] = kn.astype(BF16)
    vv = z[:, KV_WIDTH:]
    if emit_kv:
        vf_ref[...] = vv
    vt_ref[...] = vv.T.astype(BF16)

    cw = 256
    for c in range(CONV_WIDTH // cw):
        u = proj(OFF_U + c * cw, cw)
        gc = proj(OFF_C + c * cw, cw)
        cu_ref[:, c * cw:(c + 1) * cw] = gc * u

    bw = 512
    for c in range(CONV_WIDTH // bw):
        bg_ref[:, c * bw:(c + 1) * bw] = proj(OFF_B + c * bw, bw)


def _inproj(x, m3, row_fn, norm_g, w_in_bf, q_gain, k_gain, rope_tabs, seq_len, emit_kv, tm, cast_jobs=()):
    m = x.shape[0]
    n_steps = m // tm
    rope = rope_tabs is not None
    row = lambda i: (i, 0)
    const = lambda i: (0, 0)
    in_specs = [
        pl.BlockSpec((tm, D_MODEL), row),
        _mod_spec(0, row_fn),
        _mod_spec(1, row_fn),
        pl.BlockSpec((1, D_MODEL), const),
        pl.BlockSpec((D_MODEL, PROJ_WIDTH), const, pipeline_mode=pl.Buffered(1)),
        pl.BlockSpec((1, HEAD_DIM), const),
        pl.BlockSpec((1, HEAD_DIM), const),
    ]
    args = [x, m3, m3, norm_g, w_in_bf, q_gain, k_gain]
    if rope:
        tiles_per_seq = seq_len // tm
        tab_spec = pl.BlockSpec((tm, HEAD_DIM), lambda i: (i % tiles_per_seq, 0))
        in_specs += [tab_spec, tab_spec]
        args += list(rope_tabs)
    for w, _ in cast_jobs:
        in_specs.append(pl.BlockSpec((w.shape[0] // n_steps, w.shape[1]), row))
        args.append(w)
    out_specs = [
        pl.BlockSpec((tm, Q_WIDTH), row),
        pl.BlockSpec((tm, KV_WIDTH), row),
        pl.BlockSpec((KV_WIDTH, tm), lambda i: (0, i)),
    ]
    out_shape = [
        jax.ShapeDtypeStruct((m, Q_WIDTH), BF16),
        jax.ShapeDtypeStruct((m, KV_WIDTH), BF16),
        jax.ShapeDtypeStruct((KV_WIDTH, m), BF16),
    ]
    if emit_kv:
        out_specs += [pl.BlockSpec((tm, KV_WIDTH), row)] * 2
        out_shape += [jax.ShapeDtypeStruct((m, KV_WIDTH), F32)] * 2
    out_specs += [pl.BlockSpec((tm, CONV_WIDTH), row)] * 2
    out_shape += [jax.ShapeDtypeStruct((m, CONV_WIDTH), F32)] * 2
    for w, keep in cast_jobs:
        assert w.shape[0] % (n_steps * BF16_SUBLANES) == 0 and keep % 128 == 0
        out_specs.append(pl.BlockSpec((w.shape[0] // n_steps, keep), row))
        out_shape.append(jax.ShapeDtypeStruct((w.shape[0], keep), BF16))
    return pl.pallas_call(
        functools.partial(_inproj_kernel, rope=rope, emit_kv=emit_kv, n_cast=len(cast_jobs)),
        grid=(m // tm,),
        in_specs=in_specs,
        out_specs=out_specs,
        out_shape=out_shape,
        scratch_shapes=[pltpu.VMEM((tm, D_MODEL), BF16)],
        compiler_params=_params(("parallel",), 56),
        name="inproj_lat" if rope else "inproj_ctx",
    )(*args)


def _attn_kernel(*refs, tq, tk, n_tiles, has_cache):
    if has_cache:
        q_ref, k_ref, vt_ref, ck_ref, cv_ref, o_ref, s_ref, m_ref = refs
        past = ck_ref.shape[0]
    else:
        q_ref, k_ref, vt_ref, o_ref, s_ref, m_ref = refs
        past = 0
    t = pl.program_id(0)
    nt = (((1,), (1,)), ((), ()))

    seq_len = k_ref.shape[0]
    chunks = ([(0, past)] if has_cache else []) + [(past + c, tk) for c in range(0, seq_len, tk)]

    def step(do_score, do_value):
        cols = GROUPS * tq
        cw = min(cols, ATTN_COL_GROUP)
        groups = range(cols // cw)
        if do_score:
            qs = jnp.concatenate([q_ref[:, g * HEAD_DIM:(g + 1) * HEAD_DIM] for g in range(GROUPS)], axis=0)
        if do_value:
            m_old = [m_ref[:, j * cw:(j + 1) * cw] for j in groups]
        m_new = [None for _ in groups]
        l_sum = [None for _ in groups]
        acc = [None for _ in groups]
        for off, n in chunks:
            cached = off < past
            if do_value:
                vtc = cv_ref[...].T.astype(BF16) if cached else vt_ref[:, off - past:off - past + n]
            if do_score:
                kc = ck_ref[...].astype(BF16) if cached else k_ref[off - past:off - past + n, :]
            for j in groups:
                c0 = j * cw
                if do_value:
                    p = jnp.exp2(s_ref[off:off + n, c0:c0 + cw] - m_old[j])
                    ps = jnp.sum(p, axis=0, keepdims=True)
                    l_sum[j] = ps if l_sum[j] is None else l_sum[j] + ps
                    pv = jnp.dot(vtc, p.astype(BF16), preferred_element_type=F32)
                    acc[j] = pv if acc[j] is None else acc[j] + pv
                if do_score:
                    st = lax.dot_general(kc, qs[c0:c0 + cw, :], nt, preferred_element_type=F32)
                    s_ref[off:off + n, c0:c0 + cw] = st
                    cm = jnp.max(st, axis=0, keepdims=True)
                    m_new[j] = cm if m_new[j] is None else jnp.maximum(m_new[j], cm)
        for j in groups:
            c0 = j * cw
            if do_score:
                m_ref[:, c0:c0 + cw] = m_new[j]
            if do_value:
                out_t = acc[j] / l_sum[j]
                for g in range(c0 // tq, (c0 + cw) // tq):
                    lo = g * tq - c0
                    o_ref[:, g * HEAD_DIM:(g + 1) * HEAD_DIM] = out_t[:, lo:lo + tq].T.astype(BF16)

    @pl.when(t == 0)
    def _():
        step(True, False)

    @pl.when((t > 0) & (t < n_tiles))
    def _():
        step(True, True)

    @pl.when(t == n_tiles)
    def _():
        step(False, True)


def _attention(q, k, vt, cache, n_batch, seq_len, tq, tk):
    m = q.shape[0]
    nq = seq_len // tq
    n_tiles = n_batch * N_KV_HEADS * nq
    has_cache = cache is not None

    def tile(u):
        return u // (N_KV_HEADS * nq), (u // nq) % N_KV_HEADS, u % nq

    def score_tile(t):
        return tile(jnp.minimum(t, n_tiles - 1))

    def value_tile(t):
        return tile(jnp.maximum(t - 1, 0))

    def q_map(t):
        b, h, i = score_tile(t)
        return b * nq + i, h

    def k_map(t):
        b, h, _ = score_tile(t)
        return b, h

    def ck_map(t):
        b, h, _ = score_tile(t)
        return b, 0, h

    def vt_map(t):
        b, h, _ = value_tile(t)
        return h, b

    def cv_map(t):
        b, h, _ = value_tile(t)
        return b, 0, h

    def o_map(t):
        b, h, i = value_tile(t)
        return b * nq + i, h

    in_specs = [
        pl.BlockSpec((tq, GROUPS * HEAD_DIM), q_map),
        pl.BlockSpec((seq_len, HEAD_DIM), k_map),
        pl.BlockSpec((HEAD_DIM, seq_len), vt_map),
    ]
    args = [q, k, vt]
    past = 0
    if has_cache:
        past = cache[0].shape[1]
        in_specs += [pl.BlockSpec((None, past, HEAD_DIM), ck_map), pl.BlockSpec((None, past, HEAD_DIM), cv_map)]
        args += list(cache)
    cols = GROUPS * tq
    return pl.pallas_call(
        functools.partial(_attn_kernel, tq=tq, tk=tk, n_tiles=n_tiles, has_cache=has_cache),
        scratch_shapes=[pltpu.VMEM((past + seq_len, cols), F32), pltpu.VMEM((1, cols), F32)],
        grid=(n_tiles + 1,),
        in_specs=in_specs,
        out_specs=pl.BlockSpec((tq, GROUPS * HEAD_DIM), o_map),
        out_shape=jax.ShapeDtypeStruct((m, Q_WIDTH), BF16),
        compiler_params=_params(("arbitrary",), 56),
        name="attn_lat" if has_cache else "attn_ctx",
    )(*args)


def _merge_kernel(x_ref, sh_ref, sc_ref, gate_ref, g_ref, attn_ref, cu_ref, cup_ref, cun_ref, bg_ref, cw_ref,
                  wg_ref, wao_ref, wco_ref, wo_ref, o_ref, h_ref, conv_ref, merged_ref, *, tm, seq_len):

    def conv_rows():
        rows = 32
        w0, w1, w2 = cw_ref[0:1, :], cw_ref[1:2, :], cw_ref[2:3, :]
        zeros = jnp.zeros((HALO, CONV_WIDTH), F32)
        if seq_len > tm:
            tiles_per_seq = seq_len // tm
            tile_in_seq = jnp.full((HALO, CONV_WIDTH), pl.program_id(0) % tiles_per_seq, jnp.int32)
            first_halo = jnp.where(tile_in_seq == 0, 0.0, cup_ref[...])
            last_halo = jnp.where(tile_in_seq == tiles_per_seq - 1, 0.0, cun_ref[...])
        else:
            first_halo = last_halo = zeros
        for r0 in range(0, tm, rows):
            end = r0 + rows
            cur = cu_ref[r0:end, :]
            if r0 == 0:
                prev = first_halo
            elif r0 % seq_len == 0:
                prev = zeros
            else:
                prev = cu_ref[r0 - HALO:r0, :]
            if end == tm:
                nxt = last_halo
            elif end % seq_len == 0:
                nxt = zeros
            else:
                nxt = cu_ref[end:end + HALO, :]
            ext = jnp.concatenate([prev, cur, nxt], axis=0)
            up = pltpu.roll(ext, 1, axis=0)[HALO:HALO + rows]
            dn = pltpu.roll(ext, rows + 2 * HALO - 1, axis=0)[HALO:HALO + rows]
            conv = up * w0 + cur * w1 + dn * w2
            conv_ref[r0:end, :] = (bg_ref[r0:end, :] * conv).astype(BF16)

    _modulated_norm_rows(h_ref, x_ref, g_ref, sc_ref, sh_ref, tm)
    conv_rows()
    tn = 512
    for c0 in range(0, D_MODEL, tn):
        ya = jnp.dot(attn_ref[...], wao_ref[:, c0:c0 + tn], preferred_element_type=F32)
        ga = jax.nn.sigmoid(jnp.dot(h_ref[...], wg_ref[:, c0:c0 + tn], preferred_element_type=F32))
        gs = jax.nn.sigmoid(jnp.dot(h_ref[...], wg_ref[:, D_MODEL + c0:D_MODEL + c0 + tn], preferred_element_type=F32))
        yc = jnp.dot(conv_ref[...], wco_ref[:, c0:c0 + tn], preferred_element_type=F32)
        merged_ref[:, c0:c0 + tn] = (ga * ya + gs * yc).astype(BF16)
    o_ref[...] = x_ref[...] + gate_ref[...] * jnp.dot(merged_ref[...], wo_ref[...], preferred_element_type=F32)


def _merge(x, m3, row_fn, norm_g, attn, cu, bg, conv_w, w_gate, wao, wco, wo, seq_len, tm):
    m = x.shape[0]
    row = lambda i: (i, 0)
    const = lambda i: (0, 0)
    resident = functools.partial(pl.BlockSpec, index_map=const, pipeline_mode=pl.Buffered(1))
    assert seq_len % tm == 0 or tm % seq_len == 0
    blocks_per_tile = tm // HALO
    last_halo = m // HALO - 1
    in_specs = [
        pl.BlockSpec((tm, D_MODEL), row),
        _mod_spec(0, row_fn),
        _mod_spec(1, row_fn),
        _mod_spec(2, row_fn),
        pl.BlockSpec((1, D_MODEL), const),
        pl.BlockSpec((tm, Q_WIDTH), row),
        pl.BlockSpec((tm, CONV_WIDTH), row),
        pl.BlockSpec((HALO, CONV_WIDTH), lambda i: (jnp.maximum(i * blocks_per_tile - 1, 0), 0)),
        pl.BlockSpec((HALO, CONV_WIDTH), lambda i: (jnp.minimum((i + 1) * blocks_per_tile, last_halo), 0)),
        pl.BlockSpec((tm, CONV_WIDTH), row),
        pl.BlockSpec((3, CONV_WIDTH), const),
        resident((D_MODEL, 2 * D_MODEL)),
        resident((Q_WIDTH, D_MODEL)),
        resident((CONV_WIDTH, D_MODEL)),
        resident((D_MODEL, D_MODEL)),
    ]
    return pl.pallas_call(
        functools.partial(_merge_kernel, tm=tm, seq_len=seq_len),
        grid=(m // tm,),
        in_specs=in_specs,
        out_specs=pl.BlockSpec((tm, D_MODEL), row),
        out_shape=jax.ShapeDtypeStruct((m, D_MODEL), F32),
        scratch_shapes=[pltpu.VMEM((tm, D_MODEL), BF16), pltpu.VMEM((tm, CONV_WIDTH), BF16),
                        pltpu.VMEM((tm, D_MODEL), BF16)],
        compiler_params=_params(("parallel",), 56),
        name="merge",
    )(x, m3, m3, m3, norm_g, attn, cu, cu, cu, bg, conv_w, w_gate, wao, wco, wo)


def _mlp_kernel(x_ref, sh_ref, sc_ref, gate_ref, g_ref, fg_ref, w1_ref, w2_ref, o_ref, h_ref):
    fc = pl.program_id(1)
    last = pl.num_programs(1) - 1
    tm = x_ref.shape[0]

    def part():
        f = jnp.maximum(jnp.dot(h_ref[...], w1_ref[...], preferred_element_type=F32), 0.0)
        return jnp.dot((f * f).astype(BF16), w2_ref[...], preferred_element_type=F32)

    @pl.when(fc == 0)
    def _():
        _modulated_norm_rows(h_ref, x_ref, g_ref, sc_ref, sh_ref, tm)
        o_ref[...] = part()

    @pl.when((fc > 0) & (fc < last))
    def _():
        o_ref[...] += part()

    @pl.when(fc == last)
    def _():
        o_ref[...] += part()
        gate = gate_ref[...]
        fg = fg_ref[...]
        rows = 16
        for r0 in range(0, tm, rows):
            y = x_ref[r0:r0 + rows, :] + gate * o_ref[r0:r0 + rows, :]
            ms = jnp.mean(y * y, axis=-1, keepdims=True)
            o_ref[r0:r0 + rows, :] = y * lax.rsqrt(ms + EPS) * fg


def _mlp(x, m3, row_fn, norm_g, final_g, w1, w2, tm, tf):
    m = x.shape[0]
    row = lambda i, j: (i, 0)
    const = lambda i, j: (0, 0)
    in_specs = [
        pl.BlockSpec((tm, D_MODEL), row),
        _mod_spec(3, row_fn),
        _mod_spec(4, row_fn),
        _mod_spec(5, row_fn),
        pl.BlockSpec((1, D_MODEL), const),
        pl.BlockSpec((1, D_MODEL), const),
        pl.BlockSpec((D_MODEL, tf), lambda i, j: (0, j)),
        pl.BlockSpec((tf, D_MODEL), lambda i, j: (j, 0)),
    ]
    return pl.pallas_call(
        _mlp_kernel,
        grid=(m // tm, D_FF // tf),
        in_specs=in_specs,
        out_specs=pl.BlockSpec((tm, D_MODEL), row),
        out_shape=jax.ShapeDtypeStruct((m, D_MODEL), F32),
        scratch_shapes=[pltpu.VMEM((tm, D_MODEL), BF16)],
        compiler_params=_params(("parallel", "arbitrary"), 60),
        name="mlp",
    )(x, m3, m3, m3, norm_g, final_g, w1, w2)


def _rope_tables(n_tokens):
    rows = n_tokens // GRID_W
    row = jnp.broadcast_to(jnp.arange(rows)[:, None], (rows, GRID_W)).reshape(-1).astype(F32)
    col = jnp.broadcast_to(jnp.arange(GRID_W)[None, :], (rows, GRID_W)).reshape(-1).astype(F32)
    half = HEAD_DIM // 2
    inv = ROPE_THETA ** (-jnp.arange(0, half, 2, dtype=F32) / half)
    ang = jnp.concatenate([row[:, None] * inv, col[:, None] * inv], axis=-1)
    cos2 = jnp.repeat(jnp.cos(ang), 2, axis=-1)
    sign = jnp.tile(jnp.array([-1.0, 1.0], F32), half)
    sin2 = jnp.repeat(jnp.sin(ang), 2, axis=-1) * sign
    return cos2, sin2


def _project(x, m3, row_fn, seq_len, rope_tabs, emit_kv, lw, tiles, cast_jobs=()):
    xf = x.reshape(x.shape[0] * seq_len, D_MODEL)
    return _inproj(xf, m3, functools.partial(row_fn, tm=tiles["inproj"]), lw["norm1_g"], lw["w_proj"], lw["q_gain"],
                   lw["k_gain"], rope_tabs, seq_len, emit_kv, tiles["inproj"], cast_jobs)


def _finish(x, proj, m3, row_fn, seq_len, cache, emit_kv, lw, tiles):
    n_batch = x.shape[0]
    xf = x.reshape(n_batch * seq_len, D_MODEL)
    if emit_kv:
        q, k, vt, kf, vf, cu, bg = proj
    else:
        q, k, vt, cu, bg = proj
        kf = vf = None
    attn = _attention(q, k, vt, cache, n_batch, seq_len, tiles["tq"], tiles["tk"])
    x1 = _merge(xf, m3, functools.partial(row_fn, tm=tiles["merge"]), lw["norm1_g"], attn, cu, bg, lw["conv_w"],
                lw["w_gate"], lw["w_attn_out"], lw["w_conv_out"], lw["w_o"], seq_len, tiles["merge"])
    y = _mlp(x1, m3, functools.partial(row_fn, tm=tiles["mlp"]), lw["norm2_g"], lw["final_g"], lw["w_ff1"],
             lw["w_ff2"], tiles["mlp"], tiles["tf"])
    return y.reshape(x.shape), kf, vf


def _ctx_row(i, tm):
    return 0


def _lat_row(i, tm, seq_len):
    return 1 + (i * tm) // seq_len


def kernel(x_prompt, x_sample, cache_k, cache_v, c, c_ctx, w_mod, b_mod, norm1_g, norm2_g, w_in, q_gain, k_gain,
           conv_w, w_attn_out, w_conv_out, w_o, w_ff1, w_ff2, final_g):
    depth = w_mod.shape[0]
    batch, seq, _ = x_prompt.shape
    dec_batch, dec_seq, _ = x_sample.shape
    past = cache_k.shape[2]
    assert depth == 1 and 1 + dec_batch <= COND_ROWS

    cond = jnp.zeros((COND_ROWS, D_MODEL), F32).at[0].set(c_ctx).at[1:1 + dec_batch].set(c)
    rope_tabs = _rope_tables(dec_seq)

    xp, xs = x_prompt, x_sample
    new_ks, new_vs = [], []
    for l in range(depth):
        lw = {
            "norm1_g": norm1_g[l].reshape(1, D_MODEL),
            "norm2_g": norm2_g[l].reshape(1, D_MODEL),
            "final_g": final_g.reshape(1, D_MODEL),
            "q_gain": q_gain[l].reshape(1, HEAD_DIM),
            "k_gain": k_gain[l].reshape(1, HEAD_DIM),
            "conv_w": conv_w[l],
            "w_proj": w_in[l, :, :PROJ_WIDTH].astype(BF16),
        }
        m3 = _modulation(cond, w_mod[l], b_mod[l]).reshape(COND_ROWS, 1, N_MOD * D_MODEL)
        cache = (cache_k[:, l].reshape(dec_batch, past, KV_WIDTH), cache_v[:, l].reshape(dec_batch, past, KV_WIDTH))
        ctx_tiles = {"inproj": 512, "tq": seq, "tk": seq, "merge": 256, "mlp": 512, "tf": 2048}
        lat_tiles = {"inproj": 512, "tq": 512, "tk": 256, "merge": 256, "mlp": 512, "tf": 2048}
        lat_row = functools.partial(_lat_row, seq_len=dec_seq)
        later = ("w_gate", "w_attn_out", "w_conv_out", "w_o", "w_ff1", "w_ff2")
        cast_jobs = ((w_in[l], 2 * D_MODEL), (w_attn_out[l], D_MODEL), (w_conv_out[l], D_MODEL), (w_o[l], D_MODEL),
                     (w_ff1[l], D_FF), (w_ff2[l], D_MODEL))
        lat_out = _project(xs, m3, lat_row, dec_seq, rope_tabs, False, lw, lat_tiles, cast_jobs)
        lat_proj = lat_out[:len(lat_out) - len(later)]
        lw.update(zip(later, lat_out[len(lat_out) - len(later):]))
        ctx_proj = _project(xp, m3, _ctx_row, seq, None, True, lw, ctx_tiles)
        xp, kf, vf = _finish(xp, ctx_proj, m3, _ctx_row, seq, None, True, lw, ctx_tiles)
        xs, _, _ = _finish(xs, lat_proj, m3, lat_row, dec_seq, cache, False, lw, lat_tiles)
        new_ks.append(kf.reshape(batch, seq, N_KV_HEADS, HEAD_DIM))
        new_vs.append(vf.reshape(batch, seq, N_KV_HEADS, HEAD_DIM))
    return xp, xs, jnp.stack(new_ks, axis=1), jnp.stack(new_vs, axis=1)
```

```python
import functools

import jax
import jax.numpy as jnp
from jax import lax
from jax.experimental import pallas as pl
from jax.experimental.pallas import tpu as pltpu

D_MODEL = 2048
GRID_W = 64
HEAD_DIM = 128
N_HEADS = 8
N_KV_HEADS = 2
GROUPS = N_HEADS // N_KV_HEADS
Q_WIDTH = N_HEADS * HEAD_DIM
KV_WIDTH = N_KV_HEADS * HEAD_DIM
CONV_WIDTH = D_MODEL // 2
D_FF = 4 * D_MODEL
ROPE_THETA = 10000.0
EPS = 1e-6
N_MOD = 6
LOG2_E = 1.4426950408889634
ATTN_COL_GROUP = 512
COND_ROWS = 16
BF16_SUBLANES = 16
HALO = 8

OFF_Q = 0
OFF_K = Q_WIDTH
OFF_V = OFF_K + KV_WIDTH
OFF_U = OFF_V + KV_WIDTH
OFF_C = OFF_U + CONV_WIDTH
OFF_B = OFF_C + CONV_WIDTH
OFF_A = OFF_B + CONV_WIDTH
OFF_S = OFF_A + D_MODEL
IN_WIDTH = OFF_S + D_MODEL
PROJ_WIDTH = OFF_A

MIB = 1024 * 1024
F32 = jnp.float32
BF16 = jnp.bfloat16


V7X_VMEM_MIB = 64
TILES = {"inproj": 512, "attn_q": 512, "attn_k": 256, "merge": 256, "mlp": 512, "mlp_ff": 2048, "mod_cols": 1024}
VMEM_MIB = {"modulation": 40, "inproj": 56, "attn": 56, "merge": 56, "mlp": 60}
assert max(VMEM_MIB.values()) < V7X_VMEM_MIB


def _params(sem, kernel_name):
    return pltpu.CompilerParams(dimension_semantics=sem, vmem_limit_bytes=VMEM_MIB[kernel_name] * MIB)


def _mod_spec(chunk, row_fn):
    return pl.BlockSpec((None, 1, D_MODEL), lambda i, *_: (row_fn(i), 0, chunk))


def _modulated_norm(x, g, sc, sh):
    ms = jnp.mean(x * x, axis=-1, keepdims=True)
    y = x * lax.rsqrt(ms + EPS) * g
    return y * (1.0 + sc) + sh


def _modulated_norm_rows(h_ref, x_ref, g_ref, sc_ref, sh_ref, tm):
    rows = 16
    gs = g_ref[...] * (1.0 + sc_ref[...])
    sh = sh_ref[...]
    for r0 in range(0, tm, rows):
        x = x_ref[r0:r0 + rows, :]
        ms = jnp.mean(x * x, axis=-1, keepdims=True)
        h_ref[r0:r0 + rows, :] = (x * lax.rsqrt(ms + EPS) * gs + sh).astype(BF16)


def _head_norm(z, gain):
    ms = jnp.mean(z * z, axis=-1, keepdims=True)
    return z * lax.rsqrt(ms + EPS) * gain


def _rope(x, cos2, sin2):
    lane = lax.broadcasted_iota(jnp.int32, x.shape, 1)
    nxt = pltpu.roll(x, HEAD_DIM - 1, axis=1)
    prv = pltpu.roll(x, 1, axis=1)
    swapped = jnp.where(lane % 2 == 0, nxt, prv)
    return x * cos2 + swapped * sin2


def _mod_kernel(cond_ref, w_ref, b_ref, o_ref):
    cond = cond_ref[...]
    a = (cond * jax.nn.sigmoid(cond)).astype(BF16)
    o_ref[...] = jnp.dot(a, w_ref[...].astype(BF16), preferred_element_type=F32) + b_ref[...]


def _modulation(cond, w_mod, b_mod):
    tn = TILES["mod_cols"]
    n = w_mod.shape[1]
    return pl.pallas_call(
        _mod_kernel,
        grid=(n // tn,),
        in_specs=[
            pl.BlockSpec((COND_ROWS, D_MODEL), lambda j: (0, 0)),
            pl.BlockSpec((D_MODEL, tn), lambda j: (0, j)),
            pl.BlockSpec((1, tn), lambda j: (0, j)),
        ],
        out_specs=pl.BlockSpec((COND_ROWS, tn), lambda j: (0, j)),
        out_shape=jax.ShapeDtypeStruct((COND_ROWS, n), F32),
        compiler_params=_params(("arbitrary",), "modulation"),
        name="modulation",
    )(cond, w_mod, b_mod.reshape(1, n))


def _inproj_kernel(*refs, rope, emit_kv, n_cast):
    x_ref, sh_ref, sc_ref, g_ref, w_ref, qg_ref, kg_ref = refs[:7]
    pos = 7
    if rope:
        cos_ref, sin_ref = refs[pos:pos + 2]
        pos += 2
    cast_in = refs[pos:pos + n_cast]
    pos += n_cast
    q_ref, k_ref, vt_ref = refs[pos:pos + 3]
    pos += 3
    if emit_kv:
        kf_ref, vf_ref = refs[pos:pos + 2]
        pos += 2
    cu_ref, bg_ref = refs[pos:pos + 2]
    pos += 2
    cast_out = refs[pos:pos + n_cast]
    h_ref = refs[pos + n_cast]

    for src, dst in zip(cast_in, cast_out):
        dst[...] = src[:, src.shape[1] - dst.shape[1]:].astype(BF16)

    _modulated_norm_rows(h_ref, x_ref, g_ref, sc_ref, sh_ref, x_ref.shape[0])

    def proj(off, width):
        return jnp.dot(h_ref[...], w_ref[:, off:off + width], preferred_element_type=F32)

    if rope:
        cos2 = cos_ref[...]
        sin2 = sin_ref[...]
    scale = HEAD_DIM ** -0.5 * LOG2_E

    for hd in range(N_HEADS // 4):
        z = proj(OFF_Q + hd * 4 * HEAD_DIM, 4 * HEAD_DIM)
        for hh in range(4):
            qn = _head_norm(z[:, hh * HEAD_DIM:(hh + 1) * HEAD_DIM], qg_ref[...])
            if rope:
                qn = _rope(qn, cos2, sin2)
            col = (hd * 4 + hh) * HEAD_DIM
            q_ref[:, col:col + HEAD_DIM] = (qn * scale).astype(BF16)

    z = proj(OFF_K, 2 * KV_WIDTH)
    for hh in range(N_KV_HEADS):
        col = hh * HEAD_DIM
        kn = _head_norm(z[:, col:col + HEAD_DIM], kg_ref[...])
        if emit_kv:
            kf_ref[:, col:col + HEAD_DIM] = kn
        if rope:
            kn = _rope(kn, cos2, sin2)
        k_ref[:, col:col + HEAD_DIM] = kn.astype(BF16)
    vv = z[:, KV_WIDTH:]
    if emit_kv:
        vf_ref[...] = vv
    vt_ref[...] = vv.T.astype(BF16)

    cw = 256
    for c in range(CONV_WIDTH // cw):
        u = proj(OFF_U + c * cw, cw)
        gc = proj(OFF_C + c * cw, cw)
        cu_ref[:, c * cw:(c + 1) * cw] = gc * u

    bw = 512
    for c in range(CONV_WIDTH // bw):
        bg_ref[:, c * bw:(c + 1) * bw] = proj(OFF_B + c * bw, bw)


def _inproj(x, m3, row_fn, norm_g, w_in_bf, q_gain, k_gain, rope_tabs, seq_len, emit_kv, tm, cast_jobs=()):
    m = x.shape[0]
    n_steps = m // tm
    rope = rope_tabs is not None
    row = lambda i: (i, 0)
    const = lambda i: (0, 0)
    in_specs = [
        pl.BlockSpec((tm, D_MODEL), row),
        _mod_spec(0, row_fn),
        _mod_spec(1, row_fn),
        pl.BlockSpec((1, D_MODEL), const),
        pl.BlockSpec((D_MODEL, PROJ_WIDTH), const, pipeline_mode=pl.Buffered(1)),
        pl.BlockSpec((1, HEAD_DIM), const),
        pl.BlockSpec((1, HEAD_DIM), const),
    ]
    args = [x, m3, m3, norm_g, w_in_bf, q_gain, k_gain]
    if rope:
        tiles_per_seq = seq_len // tm
        tab_spec = pl.BlockSpec((tm, HEAD_DIM), lambda i: (i % tiles_per_seq, 0))
        in_specs += [tab_spec, tab_spec]
        args += list(rope_tabs)
    for w, _ in cast_jobs:
        in_specs.append(pl.BlockSpec((w.shape[0] // n_steps, w.shape[1]), row))
        args.append(w)
    out_specs = [
        pl.BlockSpec((tm, Q_WIDTH), row),
        pl.BlockSpec((tm, KV_WIDTH), row),
        pl.BlockSpec((KV_WIDTH, tm), lambda i: (0, i)),
    ]
    out_shape = [
        jax.ShapeDtypeStruct((m, Q_WIDTH), BF16),
        jax.ShapeDtypeStruct((m, KV_WIDTH), BF16),
        jax.ShapeDtypeStruct((KV_WIDTH, m), BF16),
    ]
    if emit_kv:
        out_specs += [pl.BlockSpec((tm, KV_WIDTH), row)] * 2
        out_shape += [jax.ShapeDtypeStruct((m, KV_WIDTH), F32)] * 2
    out_specs += [pl.BlockSpec((tm, CONV_WIDTH), row)] * 2
    out_shape += [jax.ShapeDtypeStruct((m, CONV_WIDTH), F32)] * 2
    for w, keep in cast_jobs:
        assert w.shape[0] % (n_steps * BF16_SUBLANES) == 0 and keep % 128 == 0
        out_specs.append(pl.BlockSpec((w.shape[0] // n_steps, keep), row))
        out_shape.append(jax.ShapeDtypeStruct((w.shape[0], keep), BF16))
    return pl.pallas_call(
        functools.partial(_inproj_kernel, rope=rope, emit_kv=emit_kv, n_cast=len(cast_jobs)),
        grid=(m // tm,),
        in_specs=in_specs,
        out_specs=out_specs,
        out_shape=out_shape,
        scratch_shapes=[pltpu.VMEM((tm, D_MODEL), BF16)],
        compiler_params=_params(("parallel",), "inproj"),
        name="inproj_lat" if rope else "inproj_ctx",
    )(*args)


def _attn_kernel(*refs, tq, tk, n_tiles, has_cache):
    if has_cache:
        q_ref, k_ref, vt_ref, ck_ref, cv_ref, o_ref, s_ref, m_ref = refs
        past = ck_ref.shape[0]
    else:
        q_ref, k_ref, vt_ref, o_ref, s_ref, m_ref = refs
        past = 0
    t = pl.program_id(0)
    nt = (((1,), (1,)), ((), ()))

    seq_len = k_ref.shape[0]
    chunks = ([(0, past)] if has_cache else []) + [(past + c, tk) for c in range(0, seq_len, tk)]

    def step(do_score, do_value):
        cols = GROUPS * tq
        cw = min(cols, ATTN_COL_GROUP)
        groups = range(cols // cw)
        if do_score:
            qs = jnp.concatenate([q_ref[:, g * HEAD_DIM:(g + 1) * HEAD_DIM] for g in range(GROUPS)], axis=0)
        if do_value:
            m_old = [m_ref[:, j * cw:(j + 1) * cw] for j in groups]
        m_new = [None for _ in groups]
        l_sum = [None for _ in groups]
        acc = [None for _ in groups]
        for off, n in chunks:
            cached = off < past
            if do_value:
                vtc = cv_ref[...].T.astype(BF16) if cached else vt_ref[:, off - past:off - past + n]
            if do_score:
                kc = ck_ref[...].astype(BF16) if cached else k_ref[off - past:off - past + n, :]
            for j in groups:
                c0 = j * cw
                if do_value:
                    p = jnp.exp2(s_ref[off:off + n, c0:c0 + cw] - m_old[j])
                    ps = jnp.sum(p, axis=0, keepdims=True)
                    l_sum[j] = ps if l_sum[j] is None else l_sum[j] + ps
                    pv = jnp.dot(vtc, p.astype(BF16), preferred_element_type=F32)
                    acc[j] = pv if acc[j] is None else acc[j] + pv
                if do_score:
                    st = lax.dot_general(kc, qs[c0:c0 + cw, :], nt, preferred_element_type=F32)
                    s_ref[off:off + n, c0:c0 + cw] = st
                    cm = jnp.max(st, axis=0, keepdims=True)
                    m_new[j] = cm if m_new[j] is None else jnp.maximum(m_new[j], cm)
        for j in groups:
            c0 = j * cw
            if do_score:
                m_ref[:, c0:c0 + cw] = m_new[j]
            if do_value:
                out_t = acc[j] / l_sum[j]
                for g in range(GROUPS):
                    lo, hi = max(c0, g * tq), min(c0 + cw, (g + 1) * tq)
                    if lo < hi:
                        o_ref[lo - g * tq:hi - g * tq, g * HEAD_DIM:(g + 1) * HEAD_DIM] = (
                            out_t[:, lo - c0:hi - c0].T.astype(BF16))

    @pl.when(t == 0)
    def _():
        step(True, False)

    @pl.when((t > 0) & (t < n_tiles))
    def _():
        step(True, True)

    @pl.when(t == n_tiles)
    def _():
        step(False, True)


def _attention(q, k, vt, cache, n_batch, seq_len, tq, tk):
    m = q.shape[0]
    nq = seq_len // tq
    n_tiles = n_batch * N_KV_HEADS * nq
    has_cache = cache is not None

    def tile(u):
        return u // (N_KV_HEADS * nq), (u // nq) % N_KV_HEADS, u % nq

    def score_tile(t):
        return tile(jnp.minimum(t, n_tiles - 1))

    def value_tile(t):
        return tile(jnp.maximum(t - 1, 0))

    def q_map(t):
        b, h, i = score_tile(t)
        return b * nq + i, h

    def k_map(t):
        b, h, _ = score_tile(t)
        return b, h

    def ck_map(t):
        b, h, _ = score_tile(t)
        return b, 0, h

    def vt_map(t):
        b, h, _ = value_tile(t)
        return h, b

    def cv_map(t):
        b, h, _ = value_tile(t)
        return b, 0, h

    def o_map(t):
        b, h, i = value_tile(t)
        return b * nq + i, h

    in_specs = [
        pl.BlockSpec((tq, GROUPS * HEAD_DIM), q_map),
        pl.BlockSpec((seq_len, HEAD_DIM), k_map),
        pl.BlockSpec((HEAD_DIM, seq_len), vt_map),
    ]
    args = [q, k, vt]
    past = 0
    if has_cache:
        past = cache[0].shape[1]
        in_specs += [pl.BlockSpec((None, past, HEAD_DIM), ck_map), pl.BlockSpec((None, past, HEAD_DIM), cv_map)]
        args += list(cache)
    cols = GROUPS * tq
    return pl.pallas_call(
        functools.partial(_attn_kernel, tq=tq, tk=tk, n_tiles=n_tiles, has_cache=has_cache),
        scratch_shapes=[pltpu.VMEM((past + seq_len, cols), F32), pltpu.VMEM((1, cols), F32)],
        grid=(n_tiles + 1,),
        in_specs=in_specs,
        out_specs=pl.BlockSpec((tq, GROUPS * HEAD_DIM), o_map),
        out_shape=jax.ShapeDtypeStruct((m, Q_WIDTH), BF16),
        compiler_params=_params(("arbitrary",), "attn"),
        name="attn_lat" if has_cache else "attn_ctx",
    )(*args)


def _merge_kernel(x_ref, sh_ref, sc_ref, gate_ref, g_ref, attn_ref, cu_ref, cup_ref, cun_ref, bg_ref, cw_ref,
                  wg_ref, wao_ref, wco_ref, wo_ref, o_ref, h_ref, conv_ref, merged_ref, *, tm, seq_len):

    def conv_rows():
        rows = 32
        w0, w1, w2 = cw_ref[0:1, :], cw_ref[1:2, :], cw_ref[2:3, :]
        zeros = jnp.zeros((HALO, CONV_WIDTH), F32)
        if seq_len > tm:
            tiles_per_seq = seq_len // tm
            tile_in_seq = jnp.full((HALO, CONV_WIDTH), pl.program_id(0) % tiles_per_seq, jnp.int32)
            first_halo = jnp.where(tile_in_seq == 0, 0.0, cup_ref[...])
            last_halo = jnp.where(tile_in_seq == tiles_per_seq - 1, 0.0, cun_ref[...])
        else:
            first_halo = last_halo = zeros
        for r0 in range(0, tm, rows):
            end = r0 + rows
            cur = cu_ref[r0:end, :]
            if r0 == 0:
                prev = first_halo
            elif r0 % seq_len == 0:
                prev = zeros
            else:
                prev = cu_ref[r0 - HALO:r0, :]
            if end == tm:
                nxt = last_halo
            elif end % seq_len == 0:
                nxt = zeros
            else:
                nxt = cu_ref[end:end + HALO, :]
            ext = jnp.concatenate([prev, cur, nxt], axis=0)
            up = pltpu.roll(ext, 1, axis=0)[HALO:HALO + rows]
            dn = pltpu.roll(ext, rows + 2 * HALO - 1, axis=0)[HALO:HALO + rows]
            conv = up * w0 + cur * w1 + dn * w2
            conv_ref[r0:end, :] = (bg_ref[r0:end, :] * conv).astype(BF16)

    _modulated_norm_rows(h_ref, x_ref, g_ref, sc_ref, sh_ref, tm)
    conv_rows()
    tn = 512
    for c0 in range(0, D_MODEL, tn):
        ya = jnp.dot(attn_ref[...], wao_ref[:, c0:c0 + tn], preferred_element_type=F32)
        ga = jax.nn.sigmoid(jnp.dot(h_ref[...], wg_ref[:, c0:c0 + tn], preferred_element_type=F32))
        gs = jax.nn.sigmoid(jnp.dot(h_ref[...], wg_ref[:, D_MODEL + c0:D_MODEL + c0 + tn], preferred_element_type=F32))
        yc = jnp.dot(conv_ref[...], wco_ref[:, c0:c0 + tn], preferred_element_type=F32)
        merged_ref[:, c0:c0 + tn] = (ga * ya + gs * yc).astype(BF16)
    o_ref[...] = x_ref[...] + gate_ref[...] * jnp.dot(merged_ref[...], wo_ref[...], preferred_element_type=F32)


def _merge(x, m3, row_fn, norm_g, attn, cu, bg, conv_w, w_gate, wao, wco, wo, seq_len, tm):
    m = x.shape[0]
    row = lambda i: (i, 0)
    const = lambda i: (0, 0)
    resident = functools.partial(pl.BlockSpec, index_map=const, pipeline_mode=pl.Buffered(1))
    assert seq_len % tm == 0 or tm % seq_len == 0
    blocks_per_tile = tm // HALO
    last_halo = m // HALO - 1
    in_specs = [
        pl.BlockSpec((tm, D_MODEL), row),
        _mod_spec(0, row_fn),
        _mod_spec(1, row_fn),
        _mod_spec(2, row_fn),
        pl.BlockSpec((1, D_MODEL), const),
        pl.BlockSpec((tm, Q_WIDTH), row),
        pl.BlockSpec((tm, CONV_WIDTH), row),
        pl.BlockSpec((HALO, CONV_WIDTH), lambda i: (jnp.maximum(i * blocks_per_tile - 1, 0), 0)),
        pl.BlockSpec((HALO, CONV_WIDTH), lambda i: (jnp.minimum((i + 1) * blocks_per_tile, last_halo), 0)),
        pl.BlockSpec((tm, CONV_WIDTH), row),
        pl.BlockSpec((3, CONV_WIDTH), const),
        resident((D_MODEL, 2 * D_MODEL)),
        resident((Q_WIDTH, D_MODEL)),
        resident((CONV_WIDTH, D_MODEL)),
        resident((D_MODEL, D_MODEL)),
    ]
    return pl.pallas_call(
        functools.partial(_merge_kernel, tm=tm, seq_len=seq_len),
        grid=(m // tm,),
        in_specs=in_specs,
        out_specs=pl.BlockSpec((tm, D_MODEL), row),
        out_shape=jax.ShapeDtypeStruct((m, D_MODEL), F32),
        scratch_shapes=[pltpu.VMEM((tm, D_MODEL), BF16), pltpu.VMEM((tm, CONV_WIDTH), BF16),
                        pltpu.VMEM((tm, D_MODEL), BF16)],
        compiler_params=_params(("parallel",), "merge"),
        name="merge",
    )(x, m3, m3, m3, norm_g, attn, cu, cu, cu, bg, conv_w, w_gate, wao, wco, wo)


def _mlp_kernel(x_ref, sh_ref, sc_ref, gate_ref, g_ref, fg_ref, w1_ref, w2_ref, o_ref, h_ref):
    fc = pl.program_id(1)
    last = pl.num_programs(1) - 1
    tm = x_ref.shape[0]

    def part():
        f = jnp.maximum(jnp.dot(h_ref[...], w1_ref[...], preferred_element_type=F32), 0.0)
        return jnp.dot((f * f).astype(BF16), w2_ref[...], preferred_element_type=F32)

    @pl.when(fc == 0)
    def _():
        _modulated_norm_rows(h_ref, x_ref, g_ref, sc_ref, sh_ref, tm)
        o_ref[...] = part()

    @pl.when((fc > 0) & (fc < last))
    def _():
        o_ref[...] += part()

    @pl.when(fc == last)
    def _():
        o_ref[...] += part()
        gate = gate_ref[...]
        fg = fg_ref[...]
        rows = 16
        for r0 in range(0, tm, rows):
            y = x_ref[r0:r0 + rows, :] + gate * o_ref[r0:r0 + rows, :]
            ms = jnp.mean(y * y, axis=-1, keepdims=True)
            o_ref[r0:r0 + rows, :] = y * lax.rsqrt(ms + EPS) * fg


def _mlp(x, m3, row_fn, norm_g, final_g, w1, w2, tm, tf):
    m = x.shape[0]
    row = lambda i, j: (i, 0)
    const = lambda i, j: (0, 0)
    in_specs = [
        pl.BlockSpec((tm, D_MODEL), row),
        _mod_spec(3, row_fn),
        _mod_spec(4, row_fn),
        _mod_spec(5, row_fn),
        pl.BlockSpec((1, D_MODEL), const),
        pl.BlockSpec((1, D_MODEL), const),
        pl.BlockSpec((D_MODEL, tf), lambda i, j: (0, j)),
        pl.BlockSpec((tf, D_MODEL), lambda i, j: (j, 0)),
    ]
    return pl.pallas_call(
        _mlp_kernel,
        grid=(m // tm, D_FF // tf),
        in_specs=in_specs,
        out_specs=pl.BlockSpec((tm, D_MODEL), row),
        out_shape=jax.ShapeDtypeStruct((m, D_MODEL), F32),
        scratch_shapes=[pltpu.VMEM((tm, D_MODEL), BF16)],
        compiler_params=_params(("parallel", "arbitrary"), "mlp"),
        name="mlp",
    )(x, m3, m3, m3, norm_g, final_g, w1, w2)


def _rope_tables(n_tokens):
    rows = n_tokens // GRID_W
    row = jnp.broadcast_to(jnp.arange(rows)[:, None], (rows, GRID_W)).reshape(-1).astype(F32)
    col = jnp.broadcast_to(jnp.arange(GRID_W)[None, :], (rows, GRID_W)).reshape(-1).astype(F32)
    half = HEAD_DIM // 2
    inv = ROPE_THETA ** (-jnp.arange(0, half, 2, dtype=F32) / half)
    ang = jnp.concatenate([row[:, None] * inv, col[:, None] * inv], axis=-1)
    cos2 = jnp.repeat(jnp.cos(ang), 2, axis=-1)
    sign = jnp.tile(jnp.array([-1.0, 1.0], F32), half)
    sin2 = jnp.repeat(jnp.sin(ang), 2, axis=-1) * sign
    return cos2, sin2


def _project(x, m3, row_fn, seq_len, rope_tabs, emit_kv, lw, cast_jobs=()):
    xf = x.reshape(x.shape[0] * seq_len, D_MODEL)
    return _inproj(xf, m3, functools.partial(row_fn, tm=TILES["inproj"]), lw["norm1_g"], lw["w_proj"], lw["q_gain"],
                   lw["k_gain"], rope_tabs, seq_len, emit_kv, TILES["inproj"], cast_jobs)


def _finish(x, proj, m3, row_fn, seq_len, cache, emit_kv, lw):
    n_batch = x.shape[0]
    xf = x.reshape(n_batch * seq_len, D_MODEL)
    if emit_kv:
        q, k, vt, kf, vf, cu, bg = proj
    else:
        q, k, vt, cu, bg = proj
        kf = vf = None
    attn = _attention(q, k, vt, cache, n_batch, seq_len, min(TILES["attn_q"], seq_len), min(TILES["attn_k"], seq_len))
    x1 = _merge(xf, m3, functools.partial(row_fn, tm=TILES["merge"]), lw["norm1_g"], attn, cu, bg, lw["conv_w"],
                lw["w_gate"], lw["w_attn_out"], lw["w_conv_out"], lw["w_o"], seq_len, TILES["merge"])
    y = _mlp(x1, m3, functools.partial(row_fn, tm=TILES["mlp"]), lw["norm2_g"], lw["final_g"], lw["w_ff1"],
             lw["w_ff2"], TILES["mlp"], TILES["mlp_ff"])
    return y.reshape(x.shape), kf, vf


def _ctx_row(i, tm):
    return 0


def _lat_row(i, tm, seq_len):
    return 1 + (i * tm) // seq_len


def kernel(x_prompt, x_sample, cache_k, cache_v, c, c_ctx, w_mod, b_mod, norm1_g, norm2_g, w_in, q_gain, k_gain,
           conv_w, w_attn_out, w_conv_out, w_o, w_ff1, w_ff2, final_g):
    depth = w_mod.shape[0]
    batch, seq, _ = x_prompt.shape
    dec_batch, dec_seq, _ = x_sample.shape
    past = cache_k.shape[2]
    assert depth == 1 and 1 + dec_batch <= COND_ROWS

    cond = jnp.zeros((COND_ROWS, D_MODEL), F32).at[0].set(c_ctx).at[1:1 + dec_batch].set(c)
    rope_tabs = _rope_tables(dec_seq)

    xp, xs = x_prompt, x_sample
    new_ks, new_vs = [], []
    for l in range(depth):
        lw = {
            "norm1_g": norm1_g[l].reshape(1, D_MODEL),
            "norm2_g": norm2_g[l].reshape(1, D_MODEL),
            "final_g": final_g.reshape(1, D_MODEL),
            "q_gain": q_gain[l].reshape(1, HEAD_DIM),
            "k_gain": k_gain[l].reshape(1, HEAD_DIM),
            "conv_w": conv_w[l],
            "w_proj": w_in[l, :, :PROJ_WIDTH].astype(BF16),
        }
        m3 = _modulation(cond, w_mod[l], b_mod[l]).reshape(COND_ROWS, 1, N_MOD * D_MODEL)
        cache = (cache_k[:, l].reshape(dec_batch, past, KV_WIDTH), cache_v[:, l].reshape(dec_batch, past, KV_WIDTH))
        lat_row = functools.partial(_lat_row, seq_len=dec_seq)
        later = ("w_gate", "w_attn_out", "w_conv_out", "w_o", "w_ff1", "w_ff2")
        cast_jobs = ((w_in[l], 2 * D_MODEL), (w_attn_out[l], D_MODEL), (w_conv_out[l], D_MODEL), (w_o[l], D_MODEL),
                     (w_ff1[l], D_FF), (w_ff2[l], D_MODEL))
        lat_out = _project(xs, m3, lat_row, dec_seq, rope_tabs, False, lw, cast_jobs)
        lat_proj = lat_out[:len(lat_out) - len(later)]
        lw.update(zip(later, lat_out[len(lat_out) - len(later):]))
        ctx_proj = _project(xp, m3, _ctx_row, seq, None, True, lw)
        xp, kf, vf = _finish(xp, ctx_proj, m3, _ctx_row, seq, None, True, lw)
        xs, _, _ = _finish(xs, lat_proj, m3, lat_row, dec_seq, cache, False, lw)
        new_ks.append(kf.reshape(batch, seq, N_KV_HEADS, HEAD_DIM))
        new_vs.append(vf.reshape(batch, seq, N_KV_HEADS, HEAD_DIM))
    return xp, xs, jnp.stack(new_ks, axis=1), jnp.stack(new_vs, axis=1)
```

```python
import functools

import jax
import jax.numpy as jnp
from jax import lax
from jax.experimental import pallas as pl
from jax.experimental.pallas import tpu as pltpu

D_MODEL = 2048
GRID_W = 64
HEAD_DIM = 128
N_HEADS = 8
N_KV_HEADS = 2
GROUPS = N_HEADS // N_KV_HEADS
Q_WIDTH = N_HEADS * HEAD_DIM
KV_WIDTH = N_KV_HEADS * HEAD_DIM
CONV_WIDTH = D_MODEL // 2
D_FF = 4 * D_MODEL
ROPE_THETA = 10000.0
EPS = 1e-6
N_MOD = 6
LOG2_E = 1.4426950408889634
ATTN_COL_GROUP = 512
COND_ROWS = 16
BF16_SUBLANES = 16
HALO = 8

OFF_Q = 0
OFF_K = Q_WIDTH
OFF_V = OFF_K + KV_WIDTH
OFF_U = OFF_V + KV_WIDTH
OFF_C = OFF_U + CONV_WIDTH
OFF_B = OFF_C + CONV_WIDTH
OFF_A = OFF_B + CONV_WIDTH
OFF_S = OFF_A + D_MODEL
IN_WIDTH = OFF_S + D_MODEL
PROJ_WIDTH = OFF_A

MIB = 1024 * 1024
F32 = jnp.float32
BF16 = jnp.bfloat16


V7X_VMEM_MIB = 64
TILES = {"inproj": 512, "attn_q": 512, "attn_k": 256, "attn_seqs": 4, "merge": 256, "mlp": 512, "mlp_ff": 2048, "mod_cols": 1024}
VMEM_MIB = {"modulation": 40, "inproj": 56, "attn": 56, "merge": 56, "mlp": 60}
assert max(VMEM_MIB.values()) < V7X_VMEM_MIB


def _params(sem, kernel_name):
    return pltpu.CompilerParams(dimension_semantics=sem, vmem_limit_bytes=VMEM_MIB[kernel_name] * MIB)


def _mod_spec(chunk, row_fn):
    return pl.BlockSpec((None, 1, D_MODEL), lambda i, *_: (row_fn(i), 0, chunk))


def _modulated_norm(x, g, sc, sh):
    ms = jnp.mean(x * x, axis=-1, keepdims=True)
    y = x * lax.rsqrt(ms + EPS) * g
    return y * (1.0 + sc) + sh


def _modulated_norm_rows(h_ref, x_ref, g_ref, sc_ref, sh_ref, tm):
    rows = 16
    gs = g_ref[...] * (1.0 + sc_ref[...])
    sh = sh_ref[...]
    for r0 in range(0, tm, rows):
        x = x_ref[r0:r0 + rows, :]
        ms = jnp.mean(x * x, axis=-1, keepdims=True)
        h_ref[r0:r0 + rows, :] = (x * lax.rsqrt(ms + EPS) * gs + sh).astype(BF16)


def _head_norm(z, gain):
    ms = jnp.mean(z * z, axis=-1, keepdims=True)
    return z * lax.rsqrt(ms + EPS) * gain


def _rope(x, cos2, sin2):
    lane = lax.broadcasted_iota(jnp.int32, x.shape, 1)
    nxt = pltpu.roll(x, HEAD_DIM - 1, axis=1)
    prv = pltpu.roll(x, 1, axis=1)
    swapped = jnp.where(lane % 2 == 0, nxt, prv)
    return x * cos2 + swapped * sin2


def _mod_kernel(cond_ref, w_ref, b_ref, o_ref):
    cond = cond_ref[...]
    a = (cond * jax.nn.sigmoid(cond)).astype(BF16)
    o_ref[...] = jnp.dot(a, w_ref[...].astype(BF16), preferred_element_type=F32) + b_ref[...]


def _modulation(cond, w_mod, b_mod):
    tn = TILES["mod_cols"]
    n = w_mod.shape[1]
    return pl.pallas_call(
        _mod_kernel,
        grid=(n // tn,),
        in_specs=[
            pl.BlockSpec((COND_ROWS, D_MODEL), lambda j: (0, 0)),
            pl.BlockSpec((D_MODEL, tn), lambda j: (0, j)),
            pl.BlockSpec((1, tn), lambda j: (0, j)),
        ],
        out_specs=pl.BlockSpec((COND_ROWS, tn), lambda j: (0, j)),
        out_shape=jax.ShapeDtypeStruct((COND_ROWS, n), F32),
        compiler_params=_params(("arbitrary",), "modulation"),
        name="modulation",
    )(cond, w_mod, b_mod.reshape(1, n))


def _inproj_kernel(*refs, rope, emit_kv, n_cast):
    x_ref, sh_ref, sc_ref, g_ref, w_ref, qg_ref, kg_ref = refs[:7]
    pos = 7
    if rope:
        cos_ref, sin_ref = refs[pos:pos + 2]
        pos += 2
    cast_in = refs[pos:pos + n_cast]
    pos += n_cast
    q_ref, k_ref, vt_ref = refs[pos:pos + 3]
    pos += 3
    if emit_kv:
        kf_ref, vf_ref = refs[pos:pos + 2]
        pos += 2
    cu_ref, bg_ref = refs[pos:pos + 2]
    pos += 2
    cast_out = refs[pos:pos + n_cast]
    h_ref = refs[pos + n_cast]

    _modulated_norm_rows(h_ref, x_ref, g_ref, sc_ref, sh_ref, x_ref.shape[0])

    def proj(off, width):
        return jnp.dot(h_ref[...], w_ref[:, off:off + width], preferred_element_type=F32)

    if rope:
        cos2 = cos_ref[...]
        sin2 = sin_ref[...]
    scale = HEAD_DIM ** -0.5 * LOG2_E

    for hd in range(N_HEADS // 4):
        z = proj(OFF_Q + hd * 4 * HEAD_DIM, 4 * HEAD_DIM)
        for hh in range(4):
            qn = _head_norm(z[:, hh * HEAD_DIM:(hh + 1) * HEAD_DIM], qg_ref[...])
            if rope:
                qn = _rope(qn, cos2, sin2)
            col = (hd * 4 + hh) * HEAD_DIM
            q_ref[:, col:col + HEAD_DIM] = (qn * scale).astype(BF16)

    z = proj(OFF_K, 2 * KV_WIDTH)
    for hh in range(N_KV_HEADS):
        col = hh * HEAD_DIM
        kn = _head_norm(z[:, col:col + HEAD_DIM], kg_ref[...])
        if emit_kv:
            kf_ref[:, col:col + HEAD_DIM] = kn
        if rope:
            kn = _rope(kn, cos2, sin2)
        k_ref[:, col:col + HEAD_DIM] = kn.astype(BF16)
    vv = z[:, KV_WIDTH:]
    if emit_kv:
        vf_ref[...] = vv
    vt_ref[...] = vv.T.astype(BF16)

    cw = 256
    for c in range(CONV_WIDTH // cw):
        u = proj(OFF_U + c * cw, cw)
        gc = proj(OFF_C + c * cw, cw)
        cu_ref[:, c * cw:(c + 1) * cw] = gc * u

    bw = 512
    for c in range(CONV_WIDTH // bw):
        bg_ref[:, c * bw:(c + 1) * bw] = proj(OFF_B + c * bw, bw)

    for src, dst in zip(cast_in, cast_out):
        dst[...] = src[:, src.shape[1] - dst.shape[1]:].astype(BF16)


def _inproj(x, m3, row_fn, norm_g, w_in_bf, q_gain, k_gain, rope_tabs, seq_len, emit_kv, tm, cast_jobs=()):
    m = x.shape[0]
    n_steps = m // tm
    rope = rope_tabs is not None
    row = lambda i: (i, 0)
    const = lambda i: (0, 0)
    in_specs = [
        pl.BlockSpec((tm, D_MODEL), row),
        _mod_spec(0, row_fn),
        _mod_spec(1, row_fn),
        pl.BlockSpec((1, D_MODEL), const),
        pl.BlockSpec((D_MODEL, PROJ_WIDTH), const, pipeline_mode=pl.Buffered(1)),
        pl.BlockSpec((1, HEAD_DIM), const),
        pl.BlockSpec((1, HEAD_DIM), const),
    ]
    args = [x, m3, m3, norm_g, w_in_bf, q_gain, k_gain]
    if rope:
        tiles_per_seq = seq_len // tm
        tab_spec = pl.BlockSpec((tm, HEAD_DIM), lambda i: (i % tiles_per_seq, 0))
        in_specs += [tab_spec, tab_spec]
        args += list(rope_tabs)
    for w, _ in cast_jobs:
        in_specs.append(pl.BlockSpec((w.shape[0] // n_steps, w.shape[1]), row))
        args.append(w)
    out_specs = [
        pl.BlockSpec((tm, Q_WIDTH), row),
        pl.BlockSpec((tm, KV_WIDTH), row),
        pl.BlockSpec((KV_WIDTH, tm), lambda i: (0, i)),
    ]
    out_shape = [
        jax.ShapeDtypeStruct((m, Q_WIDTH), BF16),
        jax.ShapeDtypeStruct((m, KV_WIDTH), BF16),
        jax.ShapeDtypeStruct((KV_WIDTH, m), BF16),
    ]
    if emit_kv:
        out_specs += [pl.BlockSpec((tm, KV_WIDTH), row)] * 2
        out_shape += [jax.ShapeDtypeStruct((m, KV_WIDTH), F32)] * 2
    out_specs += [pl.BlockSpec((tm, CONV_WIDTH), row)] * 2
    out_shape += [jax.ShapeDtypeStruct((m, CONV_WIDTH), F32)] * 2
    for w, keep in cast_jobs:
        assert w.shape[0] % (n_steps * BF16_SUBLANES) == 0 and keep % 128 == 0
        out_specs.append(pl.BlockSpec((w.shape[0] // n_steps, keep), row))
        out_shape.append(jax.ShapeDtypeStruct((w.shape[0], keep), BF16))
    return pl.pallas_call(
        functools.partial(_inproj_kernel, rope=rope, emit_kv=emit_kv, n_cast=len(cast_jobs)),
        grid=(m // tm,),
        in_specs=in_specs,
        out_specs=out_specs,
        out_shape=out_shape,
        scratch_shapes=[pltpu.VMEM((tm, D_MODEL), BF16)],
        compiler_params=_params(("parallel",), "inproj"),
        name="inproj_lat" if rope else "inproj_ctx",
    )(*args)


def _attn_kernel(*refs, tq, tk, nb, n_tiles, has_cache):
    if has_cache:
        q_ref, k_ref, vt_ref, ck_ref, cv_ref, o_ref, s_ref, m_ref = refs
        past = ck_ref.shape[0]
    else:
        q_ref, k_ref, vt_ref, o_ref, s_ref, m_ref = refs
        past = 0
    t = pl.program_id(0)
    nt = (((1,), (1,)), ((), ()))

    seq_len = k_ref.shape[0] // nb
    chunks = ([(0, past)] if has_cache else []) + [(past + c, tk) for c in range(0, seq_len, tk)]

    def step(do_score, do_value):
        cols = GROUPS * tq
        cw = min(cols, ATTN_COL_GROUP)
        units = [(b, j) for b in range(nb) for j in range(cols // cw)]
        if do_value:
            m_old = [m_ref[:, b * cols + j * cw:b * cols + (j + 1) * cw] for b, j in units]
        for u, (b, j) in enumerate(units):
            c0 = j * cw
            q0, k0, s0 = b * tq, b * seq_len, b * (past + seq_len)
            if do_score and j == 0:
                qs = jnp.concatenate([q_ref[q0:q0 + tq, g * HEAD_DIM:(g + 1) * HEAD_DIM] for g in range(GROUPS)],
                                     axis=0)
            m_new = l_sum = acc = None
            for off, n in chunks:
                cached = off < past
                kv0 = k0 + off - past
                if do_value:
                    vtc = cv_ref[...].T.astype(BF16) if cached else vt_ref[:, kv0:kv0 + n]
                    p = jnp.exp2(s_ref[s0 + off:s0 + off + n, c0:c0 + cw] - m_old[u])
                    ps = jnp.sum(p, axis=0, keepdims=True)
                    l_sum = ps if l_sum is None else l_sum + ps
                    pv = jnp.dot(vtc, p.astype(BF16), preferred_element_type=F32)
                    acc = pv if acc is None else acc + pv
                if do_score:
                    kc = ck_ref[...].astype(BF16) if cached else k_ref[kv0:kv0 + n, :]
                    st = lax.dot_general(kc, qs[c0:c0 + cw, :], nt, preferred_element_type=F32)
                    s_ref[s0 + off:s0 + off + n, c0:c0 + cw] = st
                    cm = jnp.max(st, axis=0, keepdims=True)
                    m_new = cm if m_new is None else jnp.maximum(m_new, cm)
            if do_score:
                m_ref[:, b * cols + c0:b * cols + c0 + cw] = m_new
            if do_value:
                out_t = acc / l_sum
                for g in range(GROUPS):
                    lo, hi = max(c0, g * tq), min(c0 + cw, (g + 1) * tq)
                    if lo < hi:
                        o_ref[q0 + lo - g * tq:q0 + hi - g * tq, g * HEAD_DIM:(g + 1) * HEAD_DIM] = (
                            out_t[:, lo - c0:hi - c0].T.astype(BF16))

    @pl.when(t == 0)
    def _():
        step(True, False)

    @pl.when((t > 0) & (t < n_tiles))
    def _():
        step(True, True)

    @pl.when(t == n_tiles)
    def _():
        step(False, True)


def _attention(q, k, vt, cache, n_batch, seq_len, tq, tk, nb):
    m = q.shape[0]
    nq = seq_len // tq
    has_cache = cache is not None
    assert nb == 1 or (nq == 1 and not has_cache and n_batch % nb == 0)
    n_tiles = (n_batch // nb) * N_KV_HEADS * nq

    def tile(u):
        return u // (N_KV_HEADS * nq), (u // nq) % N_KV_HEADS, u % nq

    def score_tile(t):
        return tile(jnp.minimum(t, n_tiles - 1))

    def value_tile(t):
        return tile(jnp.maximum(t - 1, 0))

    def q_map(t):
        b, h, i = score_tile(t)
        return b * nq + i, h

    def k_map(t):
        b, h, _ = score_tile(t)
        return b, h

    def ck_map(t):
        b, h, _ = score_tile(t)
        return b, 0, h

    def vt_map(t):
        b, h, _ = value_tile(t)
        return h, b

    def cv_map(t):
        b, h, _ = value_tile(t)
        return b, 0, h

    def o_map(t):
        b, h, i = value_tile(t)
        return b * nq + i, h

    in_specs = [
        pl.BlockSpec((nb * tq, GROUPS * HEAD_DIM), q_map),
        pl.BlockSpec((nb * seq_len, HEAD_DIM), k_map),
        pl.BlockSpec((HEAD_DIM, nb * seq_len), vt_map),
    ]
    args = [q, k, vt]
    past = 0
    if has_cache:
        past = cache[0].shape[1]
        in_specs += [pl.BlockSpec((None, past, HEAD_DIM), ck_map), pl.BlockSpec((None, past, HEAD_DIM), cv_map)]
        args += list(cache)
    cols = GROUPS * tq
    return pl.pallas_call(
        functools.partial(_attn_kernel, tq=tq, tk=tk, nb=nb, n_tiles=n_tiles, has_cache=has_cache),
        scratch_shapes=[pltpu.VMEM((nb * (past + seq_len), cols), F32), pltpu.VMEM((1, nb * cols), F32)],
        grid=(n_tiles + 1,),
        in_specs=in_specs,
        out_specs=pl.BlockSpec((nb * tq, GROUPS * HEAD_DIM), o_map),
        out_shape=jax.ShapeDtypeStruct((m, Q_WIDTH), BF16),
        compiler_params=_params(("arbitrary",), "attn"),
        name="attn_lat" if has_cache else "attn_ctx",
    )(*args)


def _merge_kernel(x_ref, sh_ref, sc_ref, gate_ref, g_ref, attn_ref, cu_ref, cup_ref, cun_ref, bg_ref, cw_ref,
                  wg_ref, wao_ref, wco_ref, wo_ref, o_ref, h_ref, conv_ref, merged_ref, *, tm, seq_len):

    def conv_rows():
        rows = 32
        w0, w1, w2 = cw_ref[0:1, :], cw_ref[1:2, :], cw_ref[2:3, :]
        zeros = jnp.zeros((HALO, CONV_WIDTH), F32)
        if seq_len > tm:
            tiles_per_seq = seq_len // tm
            tile_in_seq = jnp.full((HALO, CONV_WIDTH), pl.program_id(0) % tiles_per_seq, jnp.int32)
            first_halo = jnp.where(tile_in_seq == 0, 0.0, cup_ref[...])
            last_halo = jnp.where(tile_in_seq == tiles_per_seq - 1, 0.0, cun_ref[...])
        else:
            first_halo = last_halo = zeros
        for r0 in range(0, tm, rows):
            end = r0 + rows
            cur = cu_ref[r0:end, :]
            if r0 == 0:
                prev = first_halo
            elif r0 % seq_len == 0:
                prev = zeros
            else:
                prev = cu_ref[r0 - HALO:r0, :]
            if end == tm:
                nxt = last_halo
            elif end % seq_len == 0:
                nxt = zeros
            else:
                nxt = cu_ref[end:end + HALO, :]
            ext = jnp.concatenate([prev, cur, nxt], axis=0)
            up = pltpu.roll(ext, 1, axis=0)[HALO:HALO + rows]
            dn = pltpu.roll(ext, rows + 2 * HALO - 1, axis=0)[HALO:HALO + rows]
            conv = up * w0 + cur * w1 + dn * w2
            conv_ref[r0:end, :] = (bg_ref[r0:end, :] * conv).astype(BF16)

    _modulated_norm_rows(h_ref, x_ref, g_ref, sc_ref, sh_ref, tm)
    conv_rows()
    tn = 512
    for c0 in range(0, D_MODEL, tn):
        ya = jnp.dot(attn_ref[...], wao_ref[:, c0:c0 + tn], preferred_element_type=F32)
        ga = jax.nn.sigmoid(jnp.dot(h_ref[...], wg_ref[:, c0:c0 + tn], preferred_element_type=F32))
        gs = jax.nn.sigmoid(jnp.dot(h_ref[...], wg_ref[:, D_MODEL + c0:D_MODEL + c0 + tn], preferred_element_type=F32))
        yc = jnp.dot(conv_ref[...], wco_ref[:, c0:c0 + tn], preferred_element_type=F32)
        merged_ref[:, c0:c0 + tn] = (ga * ya + gs * yc).astype(BF16)
    o_ref[...] = x_ref[...] + gate_ref[...] * jnp.dot(merged_ref[...], wo_ref[...], preferred_element_type=F32)


def _merge(x, m3, row_fn, norm_g, attn, cu, bg, conv_w, w_gate, wao, wco, wo, seq_len, tm):
    m = x.shape[0]
    row = lambda i: (i, 0)
    const = lambda i: (0, 0)
    resident = functools.partial(pl.BlockSpec, index_map=const, pipeline_mode=pl.Buffered(1))
    assert seq_len % tm == 0 or tm % seq_len == 0
    blocks_per_tile = tm // HALO
    last_halo = m // HALO - 1
    in_specs = [
        pl.BlockSpec((tm, D_MODEL), row),
        _mod_spec(0, row_fn),
        _mod_spec(1, row_fn),
        _mod_spec(2, row_fn),
        pl.BlockSpec((1, D_MODEL), const),
        pl.BlockSpec((tm, Q_WIDTH), row),
        pl.BlockSpec((tm, CONV_WIDTH), row),
        pl.BlockSpec((HALO, CONV_WIDTH), lambda i: (jnp.maximum(i * blocks_per_tile - 1, 0), 0)),
        pl.BlockSpec((HALO, CONV_WIDTH), lambda i: (jnp.minimum((i + 1) * blocks_per_tile, last_halo), 0)),
        pl.BlockSpec((tm, CONV_WIDTH), row),
        pl.BlockSpec((3, CONV_WIDTH), const),
        resident((D_MODEL, 2 * D_MODEL)),
        resident((Q_WIDTH, D_MODEL)),
        resident((CONV_WIDTH, D_MODEL)),
        resident((D_MODEL, D_MODEL)),
    ]
    return pl.pallas_call(
        functools.partial(_merge_kernel, tm=tm, seq_len=seq_len),
        grid=(m // tm,),
        in_specs=in_specs,
        out_specs=pl.BlockSpec((tm, D_MODEL), row),
        out_shape=jax.ShapeDtypeStruct((m, D_MODEL), F32),
        scratch_shapes=[pltpu.VMEM((tm, D_MODEL), BF16), pltpu.VMEM((tm, CONV_WIDTH), BF16),
                        pltpu.VMEM((tm, D_MODEL), BF16)],
        compiler_params=_params(("parallel",), "merge"),
        name="merge",
    )(x, m3, m3, m3, norm_g, attn, cu, cu, cu, bg, conv_w, w_gate, wao, wco, wo)


def _mlp_kernel(x_ref, sh_ref, sc_ref, gate_ref, g_ref, fg_ref, w1_ref, w2_ref, o_ref, h_ref):
    fc = pl.program_id(1)
    last = pl.num_programs(1) - 1
    tm = x_ref.shape[0]

    def part():
        f = jnp.maximum(jnp.dot(h_ref[...], w1_ref[...], preferred_element_type=F32), 0.0)
        return jnp.dot((f * f).astype(BF16), w2_ref[...], preferred_element_type=F32)

    @pl.when(fc == 0)
    def _():
        _modulated_norm_rows(h_ref, x_ref, g_ref, sc_ref, sh_ref, tm)
        o_ref[...] = part()

    @pl.when((fc > 0) & (fc < last))
    def _():
        o_ref[...] += part()

    @pl.when(fc == last)
    def _():
        o_ref[...] += part()
        gate = gate_ref[...]
        fg = fg_ref[...]
        rows = 16
        for r0 in range(0, tm, rows):
            y = x_ref[r0:r0 + rows, :] + gate * o_ref[r0:r0 + rows, :]
            ms = jnp.mean(y * y, axis=-1, keepdims=True)
            o_ref[r0:r0 + rows, :] = y * lax.rsqrt(ms + EPS) * fg


def _mlp(x, m3, row_fn, norm_g, final_g, w1, w2, tm, tf):
    m = x.shape[0]
    row = lambda i, j: (i, 0)
    const = lambda i, j: (0, 0)
    in_specs = [
        pl.BlockSpec((tm, D_MODEL), row),
        _mod_spec(3, row_fn),
        _mod_spec(4, row_fn),
        _mod_spec(5, row_fn),
        pl.BlockSpec((1, D_MODEL), const),
        pl.BlockSpec((1, D_MODEL), const),
        pl.BlockSpec((D_MODEL, tf), lambda i, j: (0, j)),
        pl.BlockSpec((tf, D_MODEL), lambda i, j: (j, 0)),
    ]
    return pl.pallas_call(
        _mlp_kernel,
        grid=(m // tm, D_FF // tf),
        in_specs=in_specs,
        out_specs=pl.BlockSpec((tm, D_MODEL), row),
        out_shape=jax.ShapeDtypeStruct((m, D_MODEL), F32),
        scratch_shapes=[pltpu.VMEM((tm, D_MODEL), BF16)],
        compiler_params=_params(("parallel", "arbitrary"), "mlp"),
        name="mlp",
    )(x, m3, m3, m3, norm_g, final_g, w1, w2)


def _rope_tables(n_tokens):
    rows = n_tokens // GRID_W
    row = jnp.broadcast_to(jnp.arange(rows)[:, None], (rows, GRID_W)).reshape(-1).astype(F32)
    col = jnp.broadcast_to(jnp.arange(GRID_W)[None, :], (rows, GRID_W)).reshape(-1).astype(F32)
    half = HEAD_DIM // 2
    inv = ROPE_THETA ** (-jnp.arange(0, half, 2, dtype=F32) / half)
    ang = jnp.concatenate([row[:, None] * inv, col[:, None] * inv], axis=-1)
    cos2 = jnp.repeat(jnp.cos(ang), 2, axis=-1)
    sign = jnp.tile(jnp.array([-1.0, 1.0], F32), half)
    sin2 = jnp.repeat(jnp.sin(ang), 2, axis=-1) * sign
    return cos2, sin2


def _project(x, m3, row_fn, seq_len, rope_tabs, emit_kv, lw, cast_jobs=()):
    xf = x.reshape(x.shape[0] * seq_len, D_MODEL)
    return _inproj(xf, m3, functools.partial(row_fn, tm=TILES["inproj"]), lw["norm1_g"], lw["w_proj"], lw["q_gain"],
                   lw["k_gain"], rope_tabs, seq_len, emit_kv, TILES["inproj"], cast_jobs)


def _finish(x, proj, m3, row_fn, seq_len, cache, emit_kv, lw):
    n_batch = x.shape[0]
    xf = x.reshape(n_batch * seq_len, D_MODEL)
    if emit_kv:
        q, k, vt, kf, vf, cu, bg = proj
    else:
        q, k, vt, cu, bg = proj
        kf = vf = None
    tq = min(TILES["attn_q"], seq_len)
    nb = TILES["attn_seqs"] if (tq == seq_len and cache is None) else 1
    attn = _attention(q, k, vt, cache, n_batch, seq_len, tq, min(TILES["attn_k"], seq_len), nb)
    x1 = _merge(xf, m3, functools.partial(row_fn, tm=TILES["merge"]), lw["norm1_g"], attn, cu, bg, lw["conv_w"],
                lw["w_gate"], lw["w_attn_out"], lw["w_conv_out"], lw["w_o"], seq_len, TILES["merge"])
    y = _mlp(x1, m3, functools.partial(row_fn, tm=TILES["mlp"]), lw["norm2_g"], lw["final_g"], lw["w_ff1"],
             lw["w_ff2"], TILES["mlp"], TILES["mlp_ff"])
    return y.reshape(x.shape), kf, vf


def _ctx_row(i, tm):
    return 0


def _lat_row(i, tm, seq_len):
    return 1 + (i * tm) // seq_len


def kernel(x_prompt, x_sample, cache_k, cache_v, c, c_ctx, w_mod, b_mod, norm1_g, norm2_g, w_in, q_gain, k_gain,
           conv_w, w_attn_out, w_conv_out, w_o, w_ff1, w_ff2, final_g):
    depth = w_mod.shape[0]
    batch, seq, _ = x_prompt.shape
    dec_batch, dec_seq, _ = x_sample.shape
    past = cache_k.shape[2]
    assert depth == 1 and 1 + dec_batch <= COND_ROWS

    cond = jnp.zeros((COND_ROWS, D_MODEL), F32).at[0].set(c_ctx).at[1:1 + dec_batch].set(c)
    rope_tabs = _rope_tables(dec_seq)

    xp, xs = x_prompt, x_sample
    new_ks, new_vs = [], []
    for l in range(depth):
        lw = {
            "norm1_g": norm1_g[l].reshape(1, D_MODEL),
            "norm2_g": norm2_g[l].reshape(1, D_MODEL),
            "final_g": final_g.reshape(1, D_MODEL),
            "q_gain": q_gain[l].reshape(1, HEAD_DIM),
            "k_gain": k_gain[l].reshape(1, HEAD_DIM),
            "conv_w": conv_w[l],
            "w_proj": w_in[l, :, :PROJ_WIDTH].astype(BF16),
        }
        m3 = _modulation(cond, w_mod[l], b_mod[l]).reshape(COND_ROWS, 1, N_MOD * D_MODEL)
        cache = (cache_k[:, l].reshape(dec_batch, past, KV_WIDTH), cache_v[:, l].reshape(dec_batch, past, KV_WIDTH))
        lat_row = functools.partial(_lat_row, seq_len=dec_seq)
        later = ("w_gate", "w_attn_out", "w_conv_out", "w_o", "w_ff1", "w_ff2")
        cast_jobs = ((w_in[l], 2 * D_MODEL), (w_attn_out[l], D_MODEL), (w_conv_out[l], D_MODEL), (w_o[l], D_MODEL),
                     (w_ff1[l], D_FF), (w_ff2[l], D_MODEL))
        lat_out = _project(xs, m3, lat_row, dec_seq, rope_tabs, False, lw, cast_jobs)
        lat_proj = lat_out[:len(lat_out) - len(later)]
        lw.update(zip(later, lat_out[len(lat_out) - len(later):]))
        ctx_proj = _project(xp, m3, _ctx_row, seq, None, True, lw)
        xp, kf, vf = _finish(xp, ctx_proj, m3, _ctx_row, seq, None, True, lw)
        xs, _, _ = _finish(xs, lat_proj, m3, lat_row, dec_seq, cache, False, lw)
        new_ks.append(kf.reshape(batch, seq, N_KV_HEADS, HEAD_DIM))
        new_vs.append(vf.reshape(batch, seq, N_KV_HEADS, HEAD_DIM))
    return xp, xs, jnp.stack(new_ks, axis=1), jnp.stack(new_vs, axis=1)
```

```python
import functools

import jax
import jax.numpy as jnp
from jax import lax
from jax.experimental import pallas as pl
from jax.experimental.pallas import tpu as pltpu

D_MODEL = 2048
GRID_W = 64
HEAD_DIM = 128
N_HEADS = 8
N_KV_HEADS = 2
GROUPS = N_HEADS // N_KV_HEADS
Q_WIDTH = N_HEADS * HEAD_DIM
KV_WIDTH = N_KV_HEADS * HEAD_DIM
CONV_WIDTH = D_MODEL // 2
D_FF = 4 * D_MODEL
ROPE_THETA = 10000.0
EPS = 1e-6
N_MOD = 6
LOG2_E = 1.4426950408889634
ATTN_COL_GROUP = 512
COND_ROWS = 16
BF16_SUBLANES = 16
HALO = 8

OFF_Q = 0
OFF_K = Q_WIDTH
OFF_V = OFF_K + KV_WIDTH
OFF_U = OFF_V + KV_WIDTH
OFF_C = OFF_U + CONV_WIDTH
OFF_B = OFF_C + CONV_WIDTH
OFF_A = OFF_B + CONV_WIDTH
PROJ_WIDTH = OFF_A

MIB = 1024 * 1024
F32 = jnp.float32
BF16 = jnp.bfloat16


V7X_VMEM_MIB = 64
TILES = {"inproj": 512, "attn_q": 512, "attn_k": 256, "attn_seqs": 8, "merge": 256, "mlp": 512, "mlp_ff": 2048, "mod_cols": 1024}
VMEM_MIB = {"modulation": 40, "inproj": 56, "attn": 56, "merge": 56, "mlp": 60}
assert max(VMEM_MIB.values()) < V7X_VMEM_MIB


def _params(sem, kernel_name):
    return pltpu.CompilerParams(dimension_semantics=sem, vmem_limit_bytes=VMEM_MIB[kernel_name] * MIB)


def _mod_spec(chunk, row_fn):
    return pl.BlockSpec((None, 1, D_MODEL), lambda i, *_: (row_fn(i), 0, chunk))


def _modulated_norm_rows(h_ref, x_ref, g_ref, sc_ref, sh_ref, tm):
    rows = BF16_SUBLANES
    gs = g_ref[...] * (1.0 + sc_ref[...])
    sh = sh_ref[...]
    for r0 in range(0, tm, rows):
        x = x_ref[r0:r0 + rows, :]
        ms = jnp.mean(x * x, axis=-1, keepdims=True)
        h_ref[r0:r0 + rows, :] = (x * lax.rsqrt(ms + EPS) * gs + sh).astype(BF16)


def _head_norm(z, gain):
    ms = jnp.mean(z * z, axis=-1, keepdims=True)
    return z * lax.rsqrt(ms + EPS) * gain


def _rope(x, cos2, sin2):
    lane = lax.broadcasted_iota(jnp.int32, x.shape, 1)
    nxt = pltpu.roll(x, HEAD_DIM - 1, axis=1)
    prv = pltpu.roll(x, 1, axis=1)
    swapped = jnp.where(lane % 2 == 0, nxt, prv)
    return x * cos2 + swapped * sin2


def _mod_kernel(cond_ref, w_ref, b_ref, o_ref):
    cond = cond_ref[...]
    a = (cond * jax.nn.sigmoid(cond)).astype(BF16)
    o_ref[...] = jnp.dot(a, w_ref[...].astype(BF16), preferred_element_type=F32) + b_ref[...]


def _modulation(cond, w_mod, b_mod):
    tn = TILES["mod_cols"]
    n = w_mod.shape[1]
    return pl.pallas_call(
        _mod_kernel,
        grid=(n // tn,),
        in_specs=[
            pl.BlockSpec((COND_ROWS, D_MODEL), lambda j: (0, 0)),
            pl.BlockSpec((D_MODEL, tn), lambda j: (0, j)),
            pl.BlockSpec((1, tn), lambda j: (0, j)),
        ],
        out_specs=pl.BlockSpec((COND_ROWS, tn), lambda j: (0, j)),
        out_shape=jax.ShapeDtypeStruct((COND_ROWS, n), F32),
        compiler_params=_params(("arbitrary",), "modulation"),
        name="modulation",
    )(cond, w_mod, b_mod.reshape(1, n))


def _inproj_kernel(*refs, rope, emit_kv, n_cast):
    x_ref, sh_ref, sc_ref, g_ref, w_ref, qg_ref, kg_ref = refs[:7]
    pos = 7
    if rope:
        cos_ref, sin_ref = refs[pos:pos + 2]
        pos += 2
    cast_in = refs[pos:pos + n_cast]
    pos += n_cast
    q_ref, k_ref, vt_ref = refs[pos:pos + 3]
    pos += 3
    if emit_kv:
        kf_ref, vf_ref = refs[pos:pos + 2]
        pos += 2
    cu_ref, bg_ref = refs[pos:pos + 2]
    pos += 2
    cast_out = refs[pos:pos + n_cast]
    h_ref = refs[pos + n_cast]

    _modulated_norm_rows(h_ref, x_ref, g_ref, sc_ref, sh_ref, x_ref.shape[0])

    def proj(off, width):
        return jnp.dot(h_ref[...], w_ref[:, off:off + width], preferred_element_type=F32)

    if rope:
        cos2 = cos_ref[...]
        sin2 = sin_ref[...]
    scale = HEAD_DIM ** -0.5 * LOG2_E

    for hd in range(N_HEADS // 4):
        z = proj(OFF_Q + hd * 4 * HEAD_DIM, 4 * HEAD_DIM)
        for hh in range(4):
            qn = _head_norm(z[:, hh * HEAD_DIM:(hh + 1) * HEAD_DIM], qg_ref[...])
            if rope:
                qn = _rope(qn, cos2, sin2)
            col = (hd * 4 + hh) * HEAD_DIM
            q_ref[:, col:col + HEAD_DIM] = (qn * scale).astype(BF16)

    z = proj(OFF_K, 2 * KV_WIDTH)
    for hh in range(N_KV_HEADS):
        col = hh * HEAD_DIM
        kn = _head_norm(z[:, col:col + HEAD_DIM], kg_ref[...])
        if emit_kv:
            kf_ref[:, col:col + HEAD_DIM] = kn
        if rope:
            kn = _rope(kn, cos2, sin2)
        k_ref[:, col:col + HEAD_DIM] = kn.astype(BF16)
    vv = z[:, KV_WIDTH:]
    if emit_kv:
        vf_ref[...] = vv
    vt_ref[...] = vv.T.astype(BF16)

    cw = 256
    for c in range(CONV_WIDTH // cw):
        u = proj(OFF_U + c * cw, cw)
        gc = proj(OFF_C + c * cw, cw)
        cu_ref[:, c * cw:(c + 1) * cw] = gc * u

    bw = 512
    for c in range(CONV_WIDTH // bw):
        bg_ref[:, c * bw:(c + 1) * bw] = proj(OFF_B + c * bw, bw)

    for src, dst in zip(cast_in, cast_out):
        dst[...] = src[:, src.shape[1] - dst.shape[1]:].astype(BF16)


def _inproj(x, m3, row_fn, norm_g, w_in_bf, q_gain, k_gain, rope_tabs, seq_len, emit_kv, tm, cast_jobs=()):
    m = x.shape[0]
    n_steps = m // tm
    rope = rope_tabs is not None
    row = lambda i: (i, 0)
    const = lambda i: (0, 0)
    in_specs = [
        pl.BlockSpec((tm, D_MODEL), row),
        _mod_spec(0, row_fn),
        _mod_spec(1, row_fn),
        pl.BlockSpec((1, D_MODEL), const),
        pl.BlockSpec((D_MODEL, PROJ_WIDTH), const, pipeline_mode=pl.Buffered(1)),
        pl.BlockSpec((1, HEAD_DIM), const),
        pl.BlockSpec((1, HEAD_DIM), const),
    ]
    args = [x, m3, m3, norm_g, w_in_bf, q_gain, k_gain]
    if rope:
        tiles_per_seq = seq_len // tm
        tab_spec = pl.BlockSpec((tm, HEAD_DIM), lambda i: (i % tiles_per_seq, 0))
        in_specs += [tab_spec, tab_spec]
        args += list(rope_tabs)
    for w, _ in cast_jobs:
        in_specs.append(pl.BlockSpec((w.shape[0] // n_steps, w.shape[1]), row))
        args.append(w)
    out_specs = [
        pl.BlockSpec((tm, Q_WIDTH), row),
        pl.BlockSpec((tm, KV_WIDTH), row),
        pl.BlockSpec((KV_WIDTH, tm), lambda i: (0, i)),
    ]
    out_shape = [
        jax.ShapeDtypeStruct((m, Q_WIDTH), BF16),
        jax.ShapeDtypeStruct((m, KV_WIDTH), BF16),
        jax.ShapeDtypeStruct((KV_WIDTH, m), BF16),
    ]
    if emit_kv:
        out_specs += [pl.BlockSpec((tm, KV_WIDTH), row)] * 2
        out_shape += [jax.ShapeDtypeStruct((m, KV_WIDTH), F32)] * 2
    out_specs += [pl.BlockSpec((tm, CONV_WIDTH), row)] * 2
    out_shape += [jax.ShapeDtypeStruct((m, CONV_WIDTH), F32)] * 2
    for w, keep in cast_jobs:
        assert w.shape[0] % (n_steps * BF16_SUBLANES) == 0 and keep % 128 == 0
        out_specs.append(pl.BlockSpec((w.shape[0] // n_steps, keep), row))
        out_shape.append(jax.ShapeDtypeStruct((w.shape[0], keep), BF16))
    return pl.pallas_call(
        functools.partial(_inproj_kernel, rope=rope, emit_kv=emit_kv, n_cast=len(cast_jobs)),
        grid=(m // tm,),
        in_specs=in_specs,
        out_specs=out_specs,
        out_shape=out_shape,
        scratch_shapes=[pltpu.VMEM((tm, D_MODEL), BF16)],
        compiler_params=_params(("parallel",), "inproj"),
        name="inproj_lat" if rope else "inproj_ctx",
    )(*args)


def _attn_kernel(*refs, tq, tk, nb, n_tiles, has_cache):
    if has_cache:
        q_ref, k_ref, vt_ref, ck_ref, cv_ref, o_ref, s_ref, m_ref = refs
        past = ck_ref.shape[0]
    else:
        q_ref, k_ref, vt_ref, o_ref, s_ref, m_ref = refs
        past = 0
    t = pl.program_id(0)
    nt = (((1,), (1,)), ((), ()))

    seq_len = k_ref.shape[0] // nb
    chunks = ([(0, past)] if has_cache else []) + [(past + c, tk) for c in range(0, seq_len, tk)]

    def step(do_score, do_value):
        cols = GROUPS * tq
        cw = min(cols, ATTN_COL_GROUP)
        units = [(b, j) for b in range(nb) for j in range(cols // cw)]
        if do_value:
            m_old = [m_ref[:, b * cols + j * cw:b * cols + (j + 1) * cw] for b, j in units]
        for u, (b, j) in enumerate(units):
            c0 = j * cw
            q0, k0, s0 = b * tq, b * seq_len, b * (past + seq_len)
            if do_score and j == 0:
                qs = jnp.concatenate([q_ref[q0:q0 + tq, g * HEAD_DIM:(g + 1) * HEAD_DIM] for g in range(GROUPS)],
                                     axis=0)
            m_new = l_sum = acc = None
            for off, n in chunks:
                cached = off < past
                kv0 = k0 + off - past
                if do_value:
                    vtc = cv_ref[...].T.astype(BF16) if cached else vt_ref[:, kv0:kv0 + n]
                    p = jnp.exp2(s_ref[s0 + off:s0 + off + n, c0:c0 + cw] - m_old[u])
                    ps = jnp.sum(p, axis=0, keepdims=True)
                    l_sum = ps if l_sum is None else l_sum + ps
                    pv = jnp.dot(vtc, p.astype(BF16), preferred_element_type=F32)
                    acc = pv if acc is None else acc + pv
                if do_score:
                    kc = ck_ref[...].astype(BF16) if cached else k_ref[kv0:kv0 + n, :]
                    st = lax.dot_general(kc, qs[c0:c0 + cw, :], nt, preferred_element_type=F32)
                    s_ref[s0 + off:s0 + off + n, c0:c0 + cw] = st
                    cm = jnp.max(st, axis=0, keepdims=True)
                    m_new = cm if m_new is None else jnp.maximum(m_new, cm)
            if do_score:
                m_ref[:, b * cols + c0:b * cols + c0 + cw] = m_new
            if do_value:
                out_t = acc / l_sum
                for g in range(GROUPS):
                    lo, hi = max(c0, g * tq), min(c0 + cw, (g + 1) * tq)
                    if lo < hi:
                        o_ref[q0 + lo - g * tq:q0 + hi - g * tq, g * HEAD_DIM:(g + 1) * HEAD_DIM] = (
                            out_t[:, lo - c0:hi - c0].T.astype(BF16))

    @pl.when(t == 0)
    def _():
        step(True, False)

    @pl.when((t > 0) & (t < n_tiles))
    def _():
        step(True, True)

    @pl.when(t == n_tiles)
    def _():
        step(False, True)


def _attention(q, k, vt, cache, n_batch, seq_len, tq, tk, nb):
    m = q.shape[0]
    nq = seq_len // tq
    has_cache = cache is not None
    assert nb == 1 or (nq == 1 and not has_cache and n_batch % nb == 0)
    n_tiles = (n_batch // nb) * N_KV_HEADS * nq

    def tile(u):
        return u // (N_KV_HEADS * nq), (u // nq) % N_KV_HEADS, u % nq

    def score_tile(t):
        return tile(jnp.minimum(t, n_tiles - 1))

    def value_tile(t):
        return tile(jnp.maximum(t - 1, 0))

    def q_map(t):
        b, h, i = score_tile(t)
        return b * nq + i, h

    def k_map(t):
        b, h, _ = score_tile(t)
        return b, h

    def ck_map(t):
        b, h, _ = score_tile(t)
        return b, 0, h

    def vt_map(t):
        b, h, _ = value_tile(t)
        return h, b

    def cv_map(t):
        b, h, _ = value_tile(t)
        return b, 0, h

    def o_map(t):
        b, h, i = value_tile(t)
        return b * nq + i, h

    in_specs = [
        pl.BlockSpec((nb * tq, GROUPS * HEAD_DIM), q_map),
        pl.BlockSpec((nb * seq_len, HEAD_DIM), k_map),
        pl.BlockSpec((HEAD_DIM, nb * seq_len), vt_map),
    ]
    args = [q, k, vt]
    past = 0
    if has_cache:
        past = cache[0].shape[1]
        in_specs += [pl.BlockSpec((None, past, HEAD_DIM), ck_map), pl.BlockSpec((None, past, HEAD_DIM), cv_map)]
        args += list(cache)
    cols = GROUPS * tq
    return pl.pallas_call(
        functools.partial(_attn_kernel, tq=tq, tk=tk, nb=nb, n_tiles=n_tiles, has_cache=has_cache),
        scratch_shapes=[pltpu.VMEM((nb * (past + seq_len), cols), F32), pltpu.VMEM((1, nb * cols), F32)],
        grid=(n_tiles + 1,),
        in_specs=in_specs,
        out_specs=pl.BlockSpec((nb * tq, GROUPS * HEAD_DIM), o_map),
        out_shape=jax.ShapeDtypeStruct((m, Q_WIDTH), BF16),
        compiler_params=_params(("arbitrary",), "attn"),
        name="attn_lat" if has_cache else "attn_ctx",
    )(*args)


def _merge_kernel(x_ref, sh_ref, sc_ref, gate_ref, g_ref, attn_ref, cu_ref, cup_ref, cun_ref, bg_ref, cw_ref,
                  wg_ref, wao_ref, wco_ref, wo_ref, o_ref, h_ref, conv_ref, merged_ref, *, tm, seq_len):

    def conv_rows():
        rows = 32
        w0, w1, w2 = cw_ref[0:1, :], cw_ref[1:2, :], cw_ref[2:3, :]
        zeros = jnp.zeros((HALO, CONV_WIDTH), F32)
        if seq_len > tm:
            tiles_per_seq = seq_len // tm
            tile_in_seq = jnp.full((HALO, CONV_WIDTH), pl.program_id(0) % tiles_per_seq, jnp.int32)
            first_halo = jnp.where(tile_in_seq == 0, 0.0, cup_ref[...])
            last_halo = jnp.where(tile_in_seq == tiles_per_seq - 1, 0.0, cun_ref[...])
        else:
            first_halo = last_halo = zeros
        for r0 in range(0, tm, rows):
            end = r0 + rows
            cur = cu_ref[r0:end, :]
            if r0 == 0:
                prev = first_halo
            elif r0 % seq_len == 0:
                prev = zeros
            else:
                prev = cu_ref[r0 - HALO:r0, :]
            if end == tm:
                nxt = last_halo
            elif end % seq_len == 0:
                nxt = zeros
            else:
                nxt = cu_ref[end:end + HALO, :]
            ext = jnp.concatenate([prev, cur, nxt], axis=0)
            up = pltpu.roll(ext, 1, axis=0)[HALO:HALO + rows]
            dn = pltpu.roll(ext, rows + 2 * HALO - 1, axis=0)[HALO:HALO + rows]
            conv = up * w0 + cur * w1 + dn * w2
            conv_ref[r0:end, :] = (bg_ref[r0:end, :] * conv).astype(BF16)

    _modulated_norm_rows(h_ref, x_ref, g_ref, sc_ref, sh_ref, tm)
    conv_rows()
    tn = 512
    for c0 in range(0, D_MODEL, tn):
        ya = jnp.dot(attn_ref[...], wao_ref[:, c0:c0 + tn], preferred_element_type=F32)
        ga = jax.nn.sigmoid(jnp.dot(h_ref[...], wg_ref[:, c0:c0 + tn], preferred_element_type=F32))
        gs = jax.nn.sigmoid(jnp.dot(h_ref[...], wg_ref[:, D_MODEL + c0:D_MODEL + c0 + tn], preferred_element_type=F32))
        yc = jnp.dot(conv_ref[...], wco_ref[:, c0:c0 + tn], preferred_element_type=F32)
        merged_ref[:, c0:c0 + tn] = (ga * ya + gs * yc).astype(BF16)
    o_ref[...] = x_ref[...] + gate_ref[...] * jnp.dot(merged_ref[...], wo_ref[...], preferred_element_type=F32)


def _merge(x, m3, row_fn, norm_g, attn, cu, bg, conv_w, w_gate, wao, wco, wo, seq_len, tm):
    m = x.shape[0]
    row = lambda i: (i, 0)
    const = lambda i: (0, 0)
    resident = functools.partial(pl.BlockSpec, index_map=const, pipeline_mode=pl.Buffered(1))
    assert seq_len % tm == 0 or tm % seq_len == 0
    blocks_per_tile = tm // HALO
    last_halo = m // HALO - 1
    in_specs = [
        pl.BlockSpec((tm, D_MODEL), row),
        _mod_spec(0, row_fn),
        _mod_spec(1, row_fn),
        _mod_spec(2, row_fn),
        pl.BlockSpec((1, D_MODEL), const),
        pl.BlockSpec((tm, Q_WIDTH), row),
        pl.BlockSpec((tm, CONV_WIDTH), row),
        pl.BlockSpec((HALO, CONV_WIDTH), lambda i: (jnp.maximum(i * blocks_per_tile - 1, 0), 0)),
        pl.BlockSpec((HALO, CONV_WIDTH), lambda i: (jnp.minimum((i + 1) * blocks_per_tile, last_halo), 0)),
        pl.BlockSpec((tm, CONV_WIDTH), row),
        pl.BlockSpec((3, CONV_WIDTH), const),
        resident((D_MODEL, 2 * D_MODEL)),
        resident((Q_WIDTH, D_MODEL)),
        resident((CONV_WIDTH, D_MODEL)),
        resident((D_MODEL, D_MODEL)),
    ]
    return pl.pallas_call(
        functools.partial(_merge_kernel, tm=tm, seq_len=seq_len),
        grid=(m // tm,),
        in_specs=in_specs,
        out_specs=pl.BlockSpec((tm, D_MODEL), row),
        out_shape=jax.ShapeDtypeStruct((m, D_MODEL), F32),
        scratch_shapes=[pltpu.VMEM((tm, D_MODEL), BF16), pltpu.VMEM((tm, CONV_WIDTH), BF16),
                        pltpu.VMEM((tm, D_MODEL), BF16)],
        compiler_params=_params(("parallel",), "merge"),
        name="merge",
    )(x, m3, m3, m3, norm_g, attn, cu, cu, cu, bg, conv_w, w_gate, wao, wco, wo)


def _mlp_kernel(x_ref, sh_ref, sc_ref, gate_ref, g_ref, fg_ref, w1_ref, w2_ref, o_ref, h_ref):
    fc = pl.program_id(1)
    last = pl.num_programs(1) - 1
    tm = x_ref.shape[0]

    def part():
        f = jnp.maximum(jnp.dot(h_ref[...], w1_ref[...], preferred_element_type=F32), 0.0)
        return jnp.dot((f * f).astype(BF16), w2_ref[...], preferred_element_type=F32)

    @pl.when(fc == 0)
    def _():
        _modulated_norm_rows(h_ref, x_ref, g_ref, sc_ref, sh_ref, tm)
        o_ref[...] = part()

    @pl.when((fc > 0) & (fc < last))
    def _():
        o_ref[...] += part()

    @pl.when(fc == last)
    def _():
        o_ref[...] += part()
        gate = gate_ref[...]
        fg = fg_ref[...]
        rows = 16
        for r0 in range(0, tm, rows):
            y = x_ref[r0:r0 + rows, :] + gate * o_ref[r0:r0 + rows, :]
            ms = jnp.mean(y * y, axis=-1, keepdims=True)
            o_ref[r0:r0 + rows, :] = y * lax.rsqrt(ms + EPS) * fg


def _mlp(x, m3, row_fn, norm_g, final_g, w1, w2, tm, tf):
    m = x.shape[0]
    row = lambda i, j: (i, 0)
    const = lambda i, j: (0, 0)
    in_specs = [
        pl.BlockSpec((tm, D_MODEL), row),
        _mod_spec(3, row_fn),
        _mod_spec(4, row_fn),
        _mod_spec(5, row_fn),
        pl.BlockSpec((1, D_MODEL), const),
        pl.BlockSpec((1, D_MODEL), const),
        pl.BlockSpec((D_MODEL, tf), lambda i, j: (0, j)),
        pl.BlockSpec((tf, D_MODEL), lambda i, j: (j, 0)),
    ]
    return pl.pallas_call(
        _mlp_kernel,
        grid=(m // tm, D_FF // tf),
        in_specs=in_specs,
        out_specs=pl.BlockSpec((tm, D_MODEL), row),
        out_shape=jax.ShapeDtypeStruct((m, D_MODEL), F32),
        scratch_shapes=[pltpu.VMEM((tm, D_MODEL), BF16)],
        compiler_params=_params(("parallel", "arbitrary"), "mlp"),
        name="mlp",
    )(x, m3, m3, m3, norm_g, final_g, w1, w2)


def _rope_tables(n_tokens):
    rows = n_tokens // GRID_W
    row = jnp.broadcast_to(jnp.arange(rows)[:, None], (rows, GRID_W)).reshape(-1).astype(F32)
    col = jnp.broadcast_to(jnp.arange(GRID_W)[None, :], (rows, GRID_W)).reshape(-1).astype(F32)
    half = HEAD_DIM // 2
    inv = ROPE_THETA ** (-jnp.arange(0, half, 2, dtype=F32) / half)
    ang = jnp.concatenate([row[:, None] * inv, col[:, None] * inv], axis=-1)
    cos2 = jnp.repeat(jnp.cos(ang), 2, axis=-1)
    sign = jnp.tile(jnp.array([-1.0, 1.0], F32), half)
    sin2 = jnp.repeat(jnp.sin(ang), 2, axis=-1) * sign
    return cos2, sin2


def _project(x, m3, row_fn, seq_len, rope_tabs, emit_kv, lw, cast_jobs=()):
    xf = x.reshape(x.shape[0] * seq_len, D_MODEL)
    return _inproj(xf, m3, functools.partial(row_fn, tm=TILES["inproj"]), lw["norm1_g"], lw["w_proj"], lw["q_gain"],
                   lw["k_gain"], rope_tabs, seq_len, emit_kv, TILES["inproj"], cast_jobs)


def _finish(x, proj, m3, row_fn, seq_len, cache, emit_kv, lw):
    n_batch = x.shape[0]
    xf = x.reshape(n_batch * seq_len, D_MODEL)
    if emit_kv:
        q, k, vt, kf, vf, cu, bg = proj
    else:
        q, k, vt, cu, bg = proj
        kf = vf = None
    tq = min(TILES["attn_q"], seq_len)
    nb = TILES["attn_seqs"] if (tq == seq_len and cache is None) else 1
    attn = _attention(q, k, vt, cache, n_batch, seq_len, tq, min(TILES["attn_k"], seq_len), nb)
    x1 = _merge(xf, m3, functools.partial(row_fn, tm=TILES["merge"]), lw["norm1_g"], attn, cu, bg, lw["conv_w"],
                lw["w_gate"], lw["w_attn_out"], lw["w_conv_out"], lw["w_o"], seq_len, TILES["merge"])
    y = _mlp(x1, m3, functools.partial(row_fn, tm=TILES["mlp"]), lw["norm2_g"], lw["final_g"], lw["w_ff1"],
             lw["w_ff2"], TILES["mlp"], TILES["mlp_ff"])
    return y.reshape(x.shape), kf, vf


def _ctx_row(i, tm):
    return 0


def _lat_row(i, tm, seq_len):
    return 1 + (i * tm) // seq_len


def kernel(x_prompt, x_sample, cache_k, cache_v, c, c_ctx, w_mod, b_mod, norm1_g, norm2_g, w_in, q_gain, k_gain,
           conv_w, w_attn_out, w_conv_out, w_o, w_ff1, w_ff2, final_g):
    depth = w_mod.shape[0]
    batch, seq, _ = x_prompt.shape
    dec_batch, dec_seq, _ = x_sample.shape
    past = cache_k.shape[2]
    assert depth == 1 and 1 + dec_batch <= COND_ROWS

    cond = jnp.zeros((COND_ROWS, D_MODEL), F32).at[0].set(c_ctx).at[1:1 + dec_batch].set(c)
    rope_tabs = _rope_tables(dec_seq)

    xp, xs = x_prompt, x_sample
    new_ks, new_vs = [], []
    for l in range(depth):
        lw = {
            "norm1_g": norm1_g[l].reshape(1, D_MODEL),
            "norm2_g": norm2_g[l].reshape(1, D_MODEL),
            "final_g": final_g.reshape(1, D_MODEL),
            "q_gain": q_gain[l].reshape(1, HEAD_DIM),
            "k_gain": k_gain[l].reshape(1, HEAD_DIM),
            "conv_w": conv_w[l],
            "w_proj": w_in[l, :, :PROJ_WIDTH].astype(BF16),
        }
        m3 = _modulation(cond, w_mod[l], b_mod[l]).reshape(COND_ROWS, 1, N_MOD * D_MODEL)
        cache = (cache_k[:, l].reshape(dec_batch, past, KV_WIDTH), cache_v[:, l].reshape(dec_batch, past, KV_WIDTH))
        lat_row = functools.partial(_lat_row, seq_len=dec_seq)
        later = ("w_gate", "w_attn_out", "w_conv_out", "w_o", "w_ff1", "w_ff2")
        cast_jobs = ((w_in[l], 2 * D_MODEL), (w_attn_out[l], D_MODEL), (w_conv_out[l], D_MODEL), (w_o[l], D_MODEL),
                     (w_ff1[l], D_FF), (w_ff2[l], D_MODEL))
        lat_out = _project(xs, m3, lat_row, dec_seq, rope_tabs, False, lw, cast_jobs)
        lat_proj = lat_out[:len(lat_out) - len(later)]
        lw.update(zip(later, lat_out[len(lat_out) - len(later):]))
        ctx_proj = _project(xp, m3, _ctx_row, seq, None, True, lw)
        xp, kf, vf = _finish(xp, ctx_proj, m3, _ctx_row, seq, None, True, lw)
        xs, _, _ = _finish(xs, lat_proj, m3, lat_row, dec_seq, cache, False, lw)
        new_ks.append(kf.reshape(batch, seq, N_KV_HEADS, HEAD_DIM))
        new_vs.append(vf.reshape(batch, seq, N_KV_HEADS, HEAD_DIM))
    return xp, xs, jnp.stack(new_ks, axis=1), jnp.stack(new_vs, axis=1)
```

```python
import functools

import jax
import jax.numpy as jnp
from jax import lax
from jax.experimental import pallas as pl
from jax.experimental.pallas import tpu as pltpu

D_MODEL = 2048
GRID_W = 64
HEAD_DIM = 128
N_HEADS = 8
N_KV_HEADS = 2
GROUPS = N_HEADS // N_KV_HEADS
Q_WIDTH = N_HEADS * HEAD_DIM
KV_WIDTH = N_KV_HEADS * HEAD_DIM
CONV_WIDTH = D_MODEL // 2
D_FF = 4 * D_MODEL
ROPE_THETA = 10000.0
EPS = 1e-6
N_MOD = 6
LOG2_E = 1.4426950408889634
ATTN_COL_GROUP = 512
COND_ROWS = 16
BF16_SUBLANES = 16
HALO = 8

OFF_Q = 0
OFF_K = Q_WIDTH
OFF_V = OFF_K + KV_WIDTH
OFF_U = OFF_V + KV_WIDTH
OFF_C = OFF_U + CONV_WIDTH
OFF_B = OFF_C + CONV_WIDTH
OFF_A = OFF_B + CONV_WIDTH
PROJ_WIDTH = OFF_A

MIB = 1024 * 1024
F32 = jnp.float32
BF16 = jnp.bfloat16


V7X_VMEM_MIB = 64
TILES = {"inproj": 512, "attn_q": 512, "attn_k": 256, "attn_seqs": 8, "merge": 512, "mlp": 512, "mlp_ff": 2048, "mod_cols": 1024}
VMEM_MIB = {"modulation": 40, "inproj": 56, "attn": 56, "merge": 63.5, "mlp": 60}
assert max(VMEM_MIB.values()) < V7X_VMEM_MIB


def _params(sem, kernel_name):
    return pltpu.CompilerParams(dimension_semantics=sem, vmem_limit_bytes=int(VMEM_MIB[kernel_name] * MIB))


def _mod_spec(chunk, row_fn):
    return pl.BlockSpec((None, 1, D_MODEL), lambda i, *_: (row_fn(i), 0, chunk))


def _modulated_norm_rows(h_ref, x_ref, g_ref, sc_ref, sh_ref, tm):
    rows = BF16_SUBLANES
    gs = g_ref[...] * (1.0 + sc_ref[...])
    sh = sh_ref[...]
    for r0 in range(0, tm, rows):
        x = x_ref[r0:r0 + rows, :]
        ms = jnp.mean(x * x, axis=-1, keepdims=True)
        h_ref[r0:r0 + rows, :] = (x * lax.rsqrt(ms + EPS) * gs + sh).astype(BF16)


def _head_norm(z, gain):
    ms = jnp.mean(z * z, axis=-1, keepdims=True)
    return z * lax.rsqrt(ms + EPS) * gain


def _rope(x, cos2, sin2):
    lane = lax.broadcasted_iota(jnp.int32, x.shape, 1)
    nxt = pltpu.roll(x, HEAD_DIM - 1, axis=1)
    prv = pltpu.roll(x, 1, axis=1)
    swapped = jnp.where(lane % 2 == 0, nxt, prv)
    return x * cos2 + swapped * sin2


def _mod_kernel(cond_ref, w_ref, b_ref, o_ref):
    cond = cond_ref[...]
    a = (cond * jax.nn.sigmoid(cond)).astype(BF16)
    o_ref[...] = jnp.dot(a, w_ref[...].astype(BF16), preferred_element_type=F32) + b_ref[...]


def _modulation(cond, w_mod, b_mod):
    tn = TILES["mod_cols"]
    n = w_mod.shape[1]
    return pl.pallas_call(
        _mod_kernel,
        grid=(n // tn,),
        in_specs=[
            pl.BlockSpec((COND_ROWS, D_MODEL), lambda j: (0, 0)),
            pl.BlockSpec((D_MODEL, tn), lambda j: (0, j)),
            pl.BlockSpec((1, tn), lambda j: (0, j)),
        ],
        out_specs=pl.BlockSpec((COND_ROWS, tn), lambda j: (0, j)),
        out_shape=jax.ShapeDtypeStruct((COND_ROWS, n), F32),
        compiler_params=_params(("arbitrary",), "modulation"),
        name="modulation",
    )(cond, w_mod, b_mod.reshape(1, n))


def _inproj_kernel(*refs, rope, emit_kv, n_cast):
    x_ref, sh_ref, sc_ref, g_ref, w_ref, qg_ref, kg_ref = refs[:7]
    pos = 7
    if rope:
        cos_ref, sin_ref = refs[pos:pos + 2]
        pos += 2
    cast_in = refs[pos:pos + n_cast]
    pos += n_cast
    q_ref, k_ref, vt_ref = refs[pos:pos + 3]
    pos += 3
    if emit_kv:
        kf_ref, vf_ref = refs[pos:pos + 2]
        pos += 2
    cu_ref, bg_ref = refs[pos:pos + 2]
    pos += 2
    cast_out = refs[pos:pos + n_cast]
    h_ref = refs[pos + n_cast]

    _modulated_norm_rows(h_ref, x_ref, g_ref, sc_ref, sh_ref, x_ref.shape[0])

    def proj(off, width):
        return jnp.dot(h_ref[...], w_ref[:, off:off + width], preferred_element_type=F32)

    if rope:
        cos2 = cos_ref[...]
        sin2 = sin_ref[...]
    scale = HEAD_DIM ** -0.5 * LOG2_E

    for hd in range(N_HEADS // 4):
        z = proj(OFF_Q + hd * 4 * HEAD_DIM, 4 * HEAD_DIM)
        for hh in range(4):
            qn = _head_norm(z[:, hh * HEAD_DIM:(hh + 1) * HEAD_DIM], qg_ref[...])
            if rope:
                qn = _rope(qn, cos2, sin2)
            col = (hd * 4 + hh) * HEAD_DIM
            q_ref[:, col:col + HEAD_DIM] = (qn * scale).astype(BF16)

    z = proj(OFF_K, 2 * KV_WIDTH)
    for hh in range(N_KV_HEADS):
        col = hh * HEAD_DIM
        kn = _head_norm(z[:, col:col + HEAD_DIM], kg_ref[...])
        if emit_kv:
            kf_ref[:, col:col + HEAD_DIM] = kn
        if rope:
            kn = _rope(kn, cos2, sin2)
        k_ref[:, col:col + HEAD_DIM] = kn.astype(BF16)
    vv = z[:, KV_WIDTH:]
    if emit_kv:
        vf_ref[...] = vv
    vt_ref[...] = vv.T.astype(BF16)

    cw = 256
    for c in range(CONV_WIDTH // cw):
        u = proj(OFF_U + c * cw, cw)
        gc = proj(OFF_C + c * cw, cw)
        cu_ref[:, c * cw:(c + 1) * cw] = gc * u

    bw = 512
    for c in range(CONV_WIDTH // bw):
        bg_ref[:, c * bw:(c + 1) * bw] = proj(OFF_B + c * bw, bw)

    for src, dst in zip(cast_in, cast_out):
        dst[...] = src[:, src.shape[1] - dst.shape[1]:].astype(BF16)


def _inproj(x, m3, row_fn, norm_g, w_in_bf, q_gain, k_gain, rope_tabs, seq_len, emit_kv, tm, cast_jobs=()):
    m = x.shape[0]
    n_steps = m // tm
    rope = rope_tabs is not None
    row = lambda i: (i, 0)
    const = lambda i: (0, 0)
    in_specs = [
        pl.BlockSpec((tm, D_MODEL), row),
        _mod_spec(0, row_fn),
        _mod_spec(1, row_fn),
        pl.BlockSpec((1, D_MODEL), const),
        pl.BlockSpec((D_MODEL, PROJ_WIDTH), const, pipeline_mode=pl.Buffered(1)),
        pl.BlockSpec((1, HEAD_DIM), const),
        pl.BlockSpec((1, HEAD_DIM), const),
    ]
    args = [x, m3, m3, norm_g, w_in_bf, q_gain, k_gain]
    if rope:
        tiles_per_seq = seq_len // tm
        tab_spec = pl.BlockSpec((tm, HEAD_DIM), lambda i: (i % tiles_per_seq, 0))
        in_specs += [tab_spec, tab_spec]
        args += list(rope_tabs)
    for w, _ in cast_jobs:
        in_specs.append(pl.BlockSpec((w.shape[0] // n_steps, w.shape[1]), row))
        args.append(w)
    out_specs = [
        pl.BlockSpec((tm, Q_WIDTH), row),
        pl.BlockSpec((tm, KV_WIDTH), row),
        pl.BlockSpec((KV_WIDTH, tm), lambda i: (0, i)),
    ]
    out_shape = [
        jax.ShapeDtypeStruct((m, Q_WIDTH), BF16),
        jax.ShapeDtypeStruct((m, KV_WIDTH), BF16),
        jax.ShapeDtypeStruct((KV_WIDTH, m), BF16),
    ]
    if emit_kv:
        out_specs += [pl.BlockSpec((tm, KV_WIDTH), row)] * 2
        out_shape += [jax.ShapeDtypeStruct((m, KV_WIDTH), F32)] * 2
    out_specs += [pl.BlockSpec((tm, CONV_WIDTH), row)] * 2
    out_shape += [jax.ShapeDtypeStruct((m, CONV_WIDTH), F32)] * 2
    for w, keep in cast_jobs:
        assert w.shape[0] % (n_steps * BF16_SUBLANES) == 0 and keep % 128 == 0
        out_specs.append(pl.BlockSpec((w.shape[0] // n_steps, keep), row))
        out_shape.append(jax.ShapeDtypeStruct((w.shape[0], keep), BF16))
    return pl.pallas_call(
        functools.partial(_inproj_kernel, rope=rope, emit_kv=emit_kv, n_cast=len(cast_jobs)),
        grid=(m // tm,),
        in_specs=in_specs,
        out_specs=out_specs,
        out_shape=out_shape,
        scratch_shapes=[pltpu.VMEM((tm, D_MODEL), BF16)],
        compiler_params=_params(("parallel",), "inproj"),
        name="inproj_lat" if rope else "inproj_ctx",
    )(*args)


def _attn_kernel(*refs, tq, tk, nb, n_tiles, has_cache):
    if has_cache:
        q_ref, k_ref, vt_ref, ck_ref, cv_ref, o_ref, s_ref, m_ref = refs
        past = ck_ref.shape[0]
    else:
        q_ref, k_ref, vt_ref, o_ref, s_ref, m_ref = refs
        past = 0
    t = pl.program_id(0)
    nt = (((1,), (1,)), ((), ()))

    seq_len = k_ref.shape[0] // nb
    chunks = ([(0, past)] if has_cache else []) + [(past + c, tk) for c in range(0, seq_len, tk)]

    def step(do_score, do_value):
        cols = GROUPS * tq
        cw = min(cols, ATTN_COL_GROUP)
        units = [(b, j) for b in range(nb) for j in range(cols // cw)]
        if do_value:
            m_old = [m_ref[:, b * cols + j * cw:b * cols + (j + 1) * cw] for b, j in units]
        for u, (b, j) in enumerate(units):
            c0 = j * cw
            q0, k0, s0 = b * tq, b * seq_len, b * (past + seq_len)
            if do_score and j == 0:
                qs = jnp.concatenate([q_ref[q0:q0 + tq, g * HEAD_DIM:(g + 1) * HEAD_DIM] for g in range(GROUPS)],
                                     axis=0)
            m_new = l_sum = acc = None
            for off, n in chunks:
                cached = off < past
                kv0 = k0 + off - past
                if do_value:
                    vtc = cv_ref[...].T.astype(BF16) if cached else vt_ref[:, kv0:kv0 + n]
                    p = jnp.exp2(s_ref[s0 + off:s0 + off + n, c0:c0 + cw] - m_old[u])
                    ps = jnp.sum(p, axis=0, keepdims=True)
                    l_sum = ps if l_sum is None else l_sum + ps
                    pv = jnp.dot(vtc, p.astype(BF16), preferred_element_type=F32)
                    acc = pv if acc is None else acc + pv
                if do_score:
                    kc = ck_ref[...].astype(BF16) if cached else k_ref[kv0:kv0 + n, :]
                    st = lax.dot_general(kc, qs[c0:c0 + cw, :], nt, preferred_element_type=F32)
                    s_ref[s0 + off:s0 + off + n, c0:c0 + cw] = st
                    cm = jnp.max(st, axis=0, keepdims=True)
                    m_new = cm if m_new is None else jnp.maximum(m_new, cm)
            if do_score:
                m_ref[:, b * cols + c0:b * cols + c0 + cw] = m_new
            if do_value:
                out_t = acc / l_sum
                for g in range(GROUPS):
                    lo, hi = max(c0, g * tq), min(c0 + cw, (g + 1) * tq)
                    if lo < hi:
                        o_ref[q0 + lo - g * tq:q0 + hi - g * tq, g * HEAD_DIM:(g + 1) * HEAD_DIM] = (
                            out_t[:, lo - c0:hi - c0].T.astype(BF16))

    @pl.when(t == 0)
    def _():
        step(True, False)

    @pl.when((t > 0) & (t < n_tiles))
    def _():
        step(True, True)

    @pl.when(t == n_tiles)
    def _():
        step(False, True)


def _attention(q, k, vt, cache, n_batch, seq_len, tq, tk, nb):
    m = q.shape[0]
    nq = seq_len // tq
    has_cache = cache is not None
    assert nb == 1 or (nq == 1 and not has_cache and n_batch % nb == 0)
    n_tiles = (n_batch // nb) * N_KV_HEADS * nq

    def tile(u):
        return u // (N_KV_HEADS * nq), (u // nq) % N_KV_HEADS, u % nq

    def score_tile(t):
        return tile(jnp.minimum(t, n_tiles - 1))

    def value_tile(t):
        return tile(jnp.maximum(t - 1, 0))

    def q_map(t):
        b, h, i = score_tile(t)
        return b * nq + i, h

    def k_map(t):
        b, h, _ = score_tile(t)
        return b, h

    def ck_map(t):
        b, h, _ = score_tile(t)
        return b, 0, h

    def vt_map(t):
        b, h, _ = value_tile(t)
        return h, b

    def cv_map(t):
        b, h, _ = value_tile(t)
        return b, 0, h

    def o_map(t):
        b, h, i = value_tile(t)
        return b * nq + i, h

    in_specs = [
        pl.BlockSpec((nb * tq, GROUPS * HEAD_DIM), q_map),
        pl.BlockSpec((nb * seq_len, HEAD_DIM), k_map),
        pl.BlockSpec((HEAD_DIM, nb * seq_len), vt_map),
    ]
    args = [q, k, vt]
    past = 0
    if has_cache:
        past = cache[0].shape[1]
        in_specs += [pl.BlockSpec((None, past, HEAD_DIM), ck_map), pl.BlockSpec((None, past, HEAD_DIM), cv_map)]
        args += list(cache)
    cols = GROUPS * tq
    return pl.pallas_call(
        functools.partial(_attn_kernel, tq=tq, tk=tk, nb=nb, n_tiles=n_tiles, has_cache=has_cache),
        scratch_shapes=[pltpu.VMEM((nb * (past + seq_len), cols), F32), pltpu.VMEM((1, nb * cols), F32)],
        grid=(n_tiles + 1,),
        in_specs=in_specs,
        out_specs=pl.BlockSpec((nb * tq, GROUPS * HEAD_DIM), o_map),
        out_shape=jax.ShapeDtypeStruct((m, Q_WIDTH), BF16),
        compiler_params=_params(("arbitrary",), "attn"),
        name="attn_lat" if has_cache else "attn_ctx",
    )(*args)


def _merge_kernel(x_ref, sh_ref, sc_ref, gate_ref, g_ref, attn_ref, cu_ref, cup_ref, cun_ref, bg_ref, cw_ref,
                  wg_ref, wao_ref, wco_ref, wo_ref, o_ref, h_ref, conv_ref, *, tm, seq_len):

    def conv_rows():
        rows = 32
        w0, w1, w2 = cw_ref[0:1, :], cw_ref[1:2, :], cw_ref[2:3, :]
        zeros = jnp.zeros((HALO, CONV_WIDTH), F32)
        if seq_len > tm:
            tiles_per_seq = seq_len // tm
            tile_in_seq = jnp.full((HALO, CONV_WIDTH), pl.program_id(0) % tiles_per_seq, jnp.int32)
            first_halo = jnp.where(tile_in_seq == 0, 0.0, cup_ref[...])
            last_halo = jnp.where(tile_in_seq == tiles_per_seq - 1, 0.0, cun_ref[...])
        else:
            first_halo = last_halo = zeros
        for r0 in range(0, tm, rows):
            end = r0 + rows
            cur = cu_ref[r0:end, :]
            if r0 == 0:
                prev = first_halo
            elif r0 % seq_len == 0:
                prev = zeros
            else:
                prev = cu_ref[r0 - HALO:r0, :]
            if end == tm:
                nxt = last_halo
            elif end % seq_len == 0:
                nxt = zeros
            else:
                nxt = cu_ref[end:end + HALO, :]
            ext = jnp.concatenate([prev, cur, nxt], axis=0)
            up = pltpu.roll(ext, 1, axis=0)[HALO:HALO + rows]
            dn = pltpu.roll(ext, rows + 2 * HALO - 1, axis=0)[HALO:HALO + rows]
            conv = up * w0 + cur * w1 + dn * w2
            conv_ref[r0:end, :] = (bg_ref[r0:end, :] * conv).astype(BF16)

    _modulated_norm_rows(h_ref, x_ref, g_ref, sc_ref, sh_ref, tm)
    conv_rows()
    tn = 512
    for c0 in range(0, D_MODEL, tn):
        ya = jnp.dot(attn_ref[...], wao_ref[:, c0:c0 + tn], preferred_element_type=F32)
        ga = jax.nn.sigmoid(jnp.dot(h_ref[...], wg_ref[:, c0:c0 + tn], preferred_element_type=F32))
        gs = jax.nn.sigmoid(jnp.dot(h_ref[...], wg_ref[:, D_MODEL + c0:D_MODEL + c0 + tn], preferred_element_type=F32))
        yc = jnp.dot(conv_ref[...], wco_ref[:, c0:c0 + tn], preferred_element_type=F32)
        merged = (ga * ya + gs * yc).astype(BF16)
        part = jnp.dot(merged, wo_ref[c0:c0 + tn, :], preferred_element_type=F32)
        if c0 == 0:
            o_ref[...] = part
        elif c0 + tn < D_MODEL:
            o_ref[...] += part
        else:
            o_ref[...] = x_ref[...] + gate_ref[...] * (o_ref[...] + part)


def _merge(x, m3, row_fn, norm_g, attn, cu, bg, conv_w, w_gate, wao, wco, wo, seq_len, tm):
    m = x.shape[0]
    row = lambda i: (i, 0)
    const = lambda i: (0, 0)
    resident = functools.partial(pl.BlockSpec, index_map=const, pipeline_mode=pl.Buffered(1))
    assert seq_len % tm == 0 or tm % seq_len == 0
    blocks_per_tile = tm // HALO
    last_halo = m // HALO - 1
    in_specs = [
        pl.BlockSpec((tm, D_MODEL), row),
        _mod_spec(0, row_fn),
        _mod_spec(1, row_fn),
        _mod_spec(2, row_fn),
        pl.BlockSpec((1, D_MODEL), const),
        pl.BlockSpec((tm, Q_WIDTH), row),
        pl.BlockSpec((tm, CONV_WIDTH), row),
        pl.BlockSpec((HALO, CONV_WIDTH), lambda i: (jnp.maximum(i * blocks_per_tile - 1, 0), 0)),
        pl.BlockSpec((HALO, CONV_WIDTH), lambda i: (jnp.minimum((i + 1) * blocks_per_tile, last_halo), 0)),
        pl.BlockSpec((tm, CONV_WIDTH), row),
        pl.BlockSpec((3, CONV_WIDTH), const),
        resident((D_MODEL, 2 * D_MODEL)),
        resident((Q_WIDTH, D_MODEL)),
        resident((CONV_WIDTH, D_MODEL)),
        resident((D_MODEL, D_MODEL)),
    ]
    return pl.pallas_call(
        functools.partial(_merge_kernel, tm=tm, seq_len=seq_len),
        grid=(m // tm,),
        in_specs=in_specs,
        out_specs=pl.BlockSpec((tm, D_MODEL), row),
        out_shape=jax.ShapeDtypeStruct((m, D_MODEL), F32),
        scratch_shapes=[pltpu.VMEM((tm, D_MODEL), BF16), pltpu.VMEM((tm, CONV_WIDTH), BF16)],
        compiler_params=_params(("parallel",), "merge"),
        name="merge",
    )(x, m3, m3, m3, norm_g, attn, cu, cu, cu, bg, conv_w, w_gate, wao, wco, wo)


def _mlp_kernel(x_ref, sh_ref, sc_ref, gate_ref, g_ref, fg_ref, w1_ref, w2_ref, o_ref, h_ref):
    fc = pl.program_id(1)
    last = pl.num_programs(1) - 1
    tm = x_ref.shape[0]

    def part():
        f = jnp.maximum(jnp.dot(h_ref[...], w1_ref[...], preferred_element_type=F32), 0.0)
        return jnp.dot((f * f).astype(BF16), w2_ref[...], preferred_element_type=F32)

    @pl.when(fc == 0)
    def _():
        _modulated_norm_rows(h_ref, x_ref, g_ref, sc_ref, sh_ref, tm)
        o_ref[...] = part()

    @pl.when((fc > 0) & (fc < last))
    def _():
        o_ref[...] += part()

    @pl.when(fc == last)
    def _():
        o_ref[...] += part()
        gate = gate_ref[...]
        fg = fg_ref[...]
        rows = 16
        for r0 in range(0, tm, rows):
            y = x_ref[r0:r0 + rows, :] + gate * o_ref[r0:r0 + rows, :]
            ms = jnp.mean(y * y, axis=-1, keepdims=True)
            o_ref[r0:r0 + rows, :] = y * lax.rsqrt(ms + EPS) * fg


def _mlp(x, m3, row_fn, norm_g, final_g, w1, w2, tm, tf):
    m = x.shape[0]
    row = lambda i, j: (i, 0)
    const = lambda i, j: (0, 0)
    in_specs = [
        pl.BlockSpec((tm, D_MODEL), row),
        _mod_spec(3, row_fn),
        _mod_spec(4, row_fn),
        _mod_spec(5, row_fn),
        pl.BlockSpec((1, D_MODEL), const),
        pl.BlockSpec((1, D_MODEL), const),
        pl.BlockSpec((D_MODEL, tf), lambda i, j: (0, j)),
        pl.BlockSpec((tf, D_MODEL), lambda i, j: (j, 0)),
    ]
    return pl.pallas_call(
        _mlp_kernel,
        grid=(m // tm, D_FF // tf),
        in_specs=in_specs,
        out_specs=pl.BlockSpec((tm, D_MODEL), row),
        out_shape=jax.ShapeDtypeStruct((m, D_MODEL), F32),
        scratch_shapes=[pltpu.VMEM((tm, D_MODEL), BF16)],
        compiler_params=_params(("parallel", "arbitrary"), "mlp"),
        name="mlp",
    )(x, m3, m3, m3, norm_g, final_g, w1, w2)


def _rope_tables(n_tokens):
    rows = n_tokens // GRID_W
    row = jnp.broadcast_to(jnp.arange(rows)[:, None], (rows, GRID_W)).reshape(-1).astype(F32)
    col = jnp.broadcast_to(jnp.arange(GRID_W)[None, :], (rows, GRID_W)).reshape(-1).astype(F32)
    half = HEAD_DIM // 2
    inv = ROPE_THETA ** (-jnp.arange(0, half, 2, dtype=F32) / half)
    ang = jnp.concatenate([row[:, None] * inv, col[:, None] * inv], axis=-1)
    cos2 = jnp.repeat(jnp.cos(ang), 2, axis=-1)
    sign = jnp.tile(jnp.array([-1.0, 1.0], F32), half)
    sin2 = jnp.repeat(jnp.sin(ang), 2, axis=-1) * sign
    return cos2, sin2


def _project(x, m3, row_fn, seq_len, rope_tabs, emit_kv, lw, cast_jobs=()):
    xf = x.reshape(x.shape[0] * seq_len, D_MODEL)
    return _inproj(xf, m3, functools.partial(row_fn, tm=TILES["inproj"]), lw["norm1_g"], lw["w_proj"], lw["q_gain"],
                   lw["k_gain"], rope_tabs, seq_len, emit_kv, TILES["inproj"], cast_jobs)


def _finish(x, proj, m3, row_fn, seq_len, cache, emit_kv, lw):
    n_batch = x.shape[0]
    xf = x.reshape(n_batch * seq_len, D_MODEL)
    if emit_kv:
        q, k, vt, kf, vf, cu, bg = proj
    else:
        q, k, vt, cu, bg = proj
        kf = vf = None
    tq = min(TILES["attn_q"], seq_len)
    nb = TILES["attn_seqs"] if (tq == seq_len and cache is None) else 1
    attn = _attention(q, k, vt, cache, n_batch, seq_len, tq, min(TILES["attn_k"], seq_len), nb)
    x1 = _merge(xf, m3, functools.partial(row_fn, tm=TILES["merge"]), lw["norm1_g"], attn, cu, bg, lw["conv_w"],
                lw["w_gate"], lw["w_attn_out"], lw["w_conv_out"], lw["w_o"], seq_len, TILES["merge"])
    y = _mlp(x1, m3, functools.partial(row_fn, tm=TILES["mlp"]), lw["norm2_g"], lw["final_g"], lw["w_ff1"],
             lw["w_ff2"], TILES["mlp"], TILES["mlp_ff"])
    return y.reshape(x.shape), kf, vf


def _ctx_row(i, tm):
    return 0


def _lat_row(i, tm, seq_len):
    return 1 + (i * tm) // seq_len


def kernel(x_prompt, x_sample, cache_k, cache_v, c, c_ctx, w_mod, b_mod, norm1_g, norm2_g, w_in, q_gain, k_gain,
           conv_w, w_attn_out, w_conv_out, w_o, w_ff1, w_ff2, final_g):
    depth = w_mod.shape[0]
    batch, seq, _ = x_prompt.shape
    dec_batch, dec_seq, _ = x_sample.shape
    past = cache_k.shape[2]
    assert depth == 1 and 1 + dec_batch <= COND_ROWS

    cond = jnp.zeros((COND_ROWS, D_MODEL), F32).at[0].set(c_ctx).at[1:1 + dec_batch].set(c)
    rope_tabs = _rope_tables(dec_seq)

    xp, xs = x_prompt, x_sample
    new_ks, new_vs = [], []
    for l in range(depth):
        lw = {
            "norm1_g": norm1_g[l].reshape(1, D_MODEL),
            "norm2_g": norm2_g[l].reshape(1, D_MODEL),
            "final_g": final_g.reshape(1, D_MODEL),
            "q_gain": q_gain[l].reshape(1, HEAD_DIM),
            "k_gain": k_gain[l].reshape(1, HEAD_DIM),
            "conv_w": conv_w[l],
            "w_proj": w_in[l, :, :PROJ_WIDTH].astype(BF16),
        }
        m3 = _modulation(cond, w_mod[l], b_mod[l]).reshape(COND_ROWS, 1, N_MOD * D_MODEL)
        cache = (cache_k[:, l].reshape(dec_batch, past, KV_WIDTH), cache_v[:, l].reshape(dec_batch, past, KV_WIDTH))
        lat_row = functools.partial(_lat_row, seq_len=dec_seq)
        later = ("w_gate", "w_attn_out", "w_conv_out", "w_o", "w_ff1", "w_ff2")
        cast_jobs = ((w_in[l], 2 * D_MODEL), (w_attn_out[l], D_MODEL), (w_conv_out[l], D_MODEL), (w_o[l], D_MODEL),
                     (w_ff1[l], D_FF), (w_ff2[l], D_MODEL))
        lat_out = _project(xs, m3, lat_row, dec_seq, rope_tabs, False, lw, cast_jobs)
        lat_proj = lat_out[:len(lat_out) - len(later)]
        lw.update(zip(later, lat_out[len(lat_out) - len(later):]))
        ctx_proj = _project(xp, m3, _ctx_row, seq, None, True, lw)
        xp, kf, vf = _finish(xp, ctx_proj, m3, _ctx_row, seq, None, True, lw)
        xs, _, _ = _finish(xs, lat_proj, m3, lat_row, dec_seq, cache, False, lw)
        new_ks.append(kf.reshape(batch, seq, N_KV_HEADS, HEAD_DIM))
        new_vs.append(vf.reshape(batch, seq, N_KV_HEADS, HEAD_DIM))
    return xp, xs, jnp.stack(new_ks, axis=1), jnp.stack(new_vs, axis=1)
```

```python
import functools

import jax
import jax.numpy as jnp
from jax import lax
from jax.experimental import pallas as pl
from jax.experimental.pallas import tpu as pltpu

D_MODEL = 2048
GRID_W = 64
HEAD_DIM = 128
N_HEADS = 8
N_KV_HEADS = 2
GROUPS = N_HEADS // N_KV_HEADS
Q_WIDTH = N_HEADS * HEAD_DIM
KV_WIDTH = N_KV_HEADS * HEAD_DIM
CONV_WIDTH = D_MODEL // 2
D_FF = 4 * D_MODEL
ROPE_THETA = 10000.0
EPS = 1e-6
N_MOD = 6
LOG2_E = 1.4426950408889634
ATTN_COL_GROUP = 512
COND_ROWS = 16
BF16_SUBLANES = 16
HALO = 8

OFF_Q = 0
OFF_K = Q_WIDTH
OFF_V = OFF_K + KV_WIDTH
OFF_U = OFF_V + KV_WIDTH
OFF_C = OFF_U + CONV_WIDTH
OFF_B = OFF_C + CONV_WIDTH
OFF_A = OFF_B + CONV_WIDTH
PROJ_WIDTH = OFF_A

MIB = 1024 * 1024
F32 = jnp.float32
BF16 = jnp.bfloat16


V7X_VMEM_MIB = 64
TILES = {"inproj": 512, "attn_q": 512, "attn_k": 256, "attn_seqs": 8, "merge": 512, "mlp": 512, "mlp_ff": 2048, "mod_cols": 1024}
VMEM_MIB = {"modulation": 40, "inproj": 56, "attn": 56, "merge": 63.5, "mlp": 60}
assert max(VMEM_MIB.values()) < V7X_VMEM_MIB


def _params(sem, kernel_name):
    return pltpu.CompilerParams(dimension_semantics=sem, vmem_limit_bytes=int(VMEM_MIB[kernel_name] * MIB))


def _mod_spec(chunk, row_fn):
    return pl.BlockSpec((None, 1, D_MODEL), lambda i, *_: (row_fn(i), 0, chunk))


def _modulated_norm_rows(h_ref, x_ref, g_ref, sc_ref, sh_ref, tm):
    rows = BF16_SUBLANES
    gs = g_ref[...] * (1.0 + sc_ref[...])
    sh = sh_ref[...]
    for r0 in range(0, tm, rows):
        x = x_ref[r0:r0 + rows, :]
        ms = jnp.mean(x * x, axis=-1, keepdims=True)
        h_ref[r0:r0 + rows, :] = (x * lax.rsqrt(ms + EPS) * gs + sh).astype(BF16)


def _head_norm(z, gain):
    ms = jnp.mean(z * z, axis=-1, keepdims=True)
    return z * lax.rsqrt(ms + EPS) * gain


def _rope(x, cos2, sin2):
    lane = lax.broadcasted_iota(jnp.int32, x.shape, 1)
    nxt = pltpu.roll(x, HEAD_DIM - 1, axis=1)
    prv = pltpu.roll(x, 1, axis=1)
    swapped = jnp.where(lane % 2 == 0, nxt, prv)
    return x * cos2 + swapped * sin2


def _mod_kernel(cond_ref, w_ref, b_ref, o_ref):
    cond = cond_ref[...]
    a = (cond * jax.nn.sigmoid(cond)).astype(BF16)
    o_ref[...] = jnp.dot(a, w_ref[...].astype(BF16), preferred_element_type=F32) + b_ref[...]


def _modulation(cond, w_mod, b_mod):
    tn = TILES["mod_cols"]
    n = w_mod.shape[1]
    return pl.pallas_call(
        _mod_kernel,
        grid=(n // tn,),
        in_specs=[
            pl.BlockSpec((COND_ROWS, D_MODEL), lambda j: (0, 0)),
            pl.BlockSpec((D_MODEL, tn), lambda j: (0, j)),
            pl.BlockSpec((1, tn), lambda j: (0, j)),
        ],
        out_specs=pl.BlockSpec((COND_ROWS, tn), lambda j: (0, j)),
        out_shape=jax.ShapeDtypeStruct((COND_ROWS, n), F32),
        compiler_params=_params(("arbitrary",), "modulation"),
        name="modulation",
    )(cond, w_mod, b_mod.reshape(1, n))


def _inproj_kernel(*refs, rope, emit_kv, n_cast):
    x_ref, sh_ref, sc_ref, g_ref, w_ref, qg_ref, kg_ref = refs[:7]
    pos = 7
    if rope:
        cos_ref, sin_ref = refs[pos:pos + 2]
        pos += 2
    cast_in = refs[pos:pos + n_cast]
    pos += n_cast
    q_ref, k_ref, vt_ref = refs[pos:pos + 3]
    pos += 3
    if emit_kv:
        kf_ref, vf_ref = refs[pos:pos + 2]
        pos += 2
    cu_ref, bg_ref = refs[pos:pos + 2]
    pos += 2
    cast_out = refs[pos:pos + n_cast]
    h_ref = refs[pos + n_cast]

    _modulated_norm_rows(h_ref, x_ref, g_ref, sc_ref, sh_ref, x_ref.shape[0])

    def proj(off, width):
        return jnp.dot(h_ref[...], w_ref[:, off:off + width], preferred_element_type=F32)

    if rope:
        cos2 = cos_ref[...]
        sin2 = sin_ref[...]
    scale = HEAD_DIM ** -0.5 * LOG2_E

    for hd in range(N_HEADS // 4):
        z = proj(OFF_Q + hd * 4 * HEAD_DIM, 4 * HEAD_DIM)
        for hh in range(4):
            qn = _head_norm(z[:, hh * HEAD_DIM:(hh + 1) * HEAD_DIM], qg_ref[...])
            if rope:
                qn = _rope(qn, cos2, sin2)
            col = (hd * 4 + hh) * HEAD_DIM
            q_ref[:, col:col + HEAD_DIM] = (qn * scale).astype(BF16)

    z = proj(OFF_K, 2 * KV_WIDTH)
    for hh in range(N_KV_HEADS):
        col = hh * HEAD_DIM
        kn = _head_norm(z[:, col:col + HEAD_DIM], kg_ref[...])
        if emit_kv:
            kf_ref[:, col:col + HEAD_DIM] = kn
        if rope:
            kn = _rope(kn, cos2, sin2)
        k_ref[:, col:col + HEAD_DIM] = kn.astype(BF16)
    vv = z[:, KV_WIDTH:]
    if emit_kv:
        vf_ref[...] = vv
    vt_ref[...] = vv.T.astype(BF16)

    cw = 256
    for c in range(CONV_WIDTH // cw):
        u = proj(OFF_U + c * cw, cw)
        gc = proj(OFF_C + c * cw, cw)
        cu_ref[:, c * cw:(c + 1) * cw] = gc * u

    bw = 512
    for c in range(CONV_WIDTH // bw):
        bg_ref[:, c * bw:(c + 1) * bw] = proj(OFF_B + c * bw, bw)

    for src, dst in zip(cast_in, cast_out):
        dst[...] = src[:, src.shape[1] - dst.shape[1]:].astype(BF16)


def _inproj(x, m3, row_fn, norm_g, w_in_bf, q_gain, k_gain, rope_tabs, seq_len, emit_kv, tm, cast_jobs=()):
    m = x.shape[0]
    n_steps = m // tm
    rope = rope_tabs is not None
    row = lambda i: (i, 0)
    const = lambda i: (0, 0)
    in_specs = [
        pl.BlockSpec((tm, D_MODEL), row),
        _mod_spec(0, row_fn),
        _mod_spec(1, row_fn),
        pl.BlockSpec((1, D_MODEL), const),
        pl.BlockSpec((D_MODEL, PROJ_WIDTH), const, pipeline_mode=pl.Buffered(1)),
        pl.BlockSpec((1, HEAD_DIM), const),
        pl.BlockSpec((1, HEAD_DIM), const),
    ]
    args = [x, m3, m3, norm_g, w_in_bf, q_gain, k_gain]
    if rope:
        tiles_per_seq = seq_len // tm
        tab_spec = pl.BlockSpec((tm, HEAD_DIM), lambda i: (i % tiles_per_seq, 0))
        in_specs += [tab_spec, tab_spec]
        args += list(rope_tabs)
    for w, _ in cast_jobs:
        in_specs.append(pl.BlockSpec((w.shape[0] // n_steps, w.shape[1]), row))
        args.append(w)
    out_specs = [
        pl.BlockSpec((tm, Q_WIDTH), row),
        pl.BlockSpec((tm, KV_WIDTH), row),
        pl.BlockSpec((KV_WIDTH, tm), lambda i: (0, i)),
    ]
    out_shape = [
        jax.ShapeDtypeStruct((m, Q_WIDTH), BF16),
        jax.ShapeDtypeStruct((m, KV_WIDTH), BF16),
        jax.ShapeDtypeStruct((KV_WIDTH, m), BF16),
    ]
    if emit_kv:
        out_specs += [pl.BlockSpec((tm, KV_WIDTH), row)] * 2
        out_shape += [jax.ShapeDtypeStruct((m, KV_WIDTH), F32)] * 2
    out_specs += [pl.BlockSpec((tm, CONV_WIDTH), row)] * 2
    out_shape += [jax.ShapeDtypeStruct((m, CONV_WIDTH), F32)] * 2
    for w, keep in cast_jobs:
        assert w.shape[0] % (n_steps * BF16_SUBLANES) == 0 and keep % 128 == 0
        out_specs.append(pl.BlockSpec((w.shape[0] // n_steps, keep), row))
        out_shape.append(jax.ShapeDtypeStruct((w.shape[0], keep), BF16))
    return pl.pallas_call(
        functools.partial(_inproj_kernel, rope=rope, emit_kv=emit_kv, n_cast=len(cast_jobs)),
        grid=(m // tm,),
        in_specs=in_specs,
        out_specs=out_specs,
        out_shape=out_shape,
        scratch_shapes=[pltpu.VMEM((tm, D_MODEL), BF16)],
        compiler_params=_params(("parallel",), "inproj"),
        name="inproj_lat" if rope else "inproj_ctx",
    )(*args)


def _attn_kernel(*refs, tq, tk, nb, n_tiles, has_cache):
    if has_cache:
        q_ref, k_ref, vt_ref, ck_ref, cv_ref, o_ref, s_ref, m_ref = refs
        past = ck_ref.shape[0]
    else:
        q_ref, k_ref, vt_ref, o_ref, s_ref, m_ref = refs
        past = 0
    t = pl.program_id(0)
    nt = (((1,), (1,)), ((), ()))

    seq_len = k_ref.shape[0] // nb
    chunks = ([(0, past)] if has_cache else []) + [(past + c, tk) for c in range(0, seq_len, tk)]

    def step(do_score, do_value):
        cols = GROUPS * tq
        cw = min(cols, ATTN_COL_GROUP)
        units = [(b, j) for b in range(nb) for j in range(cols // cw)]
        if do_value:
            m_old = [m_ref[:, b * cols + j * cw:b * cols + (j + 1) * cw] for b, j in units]
        for u, (b, j) in enumerate(units):
            c0 = j * cw
            q0, k0, s0 = b * tq, b * seq_len, b * (past + seq_len)
            if do_score and j == 0:
                qs = jnp.concatenate([q_ref[q0:q0 + tq, g * HEAD_DIM:(g + 1) * HEAD_DIM] for g in range(GROUPS)],
                                     axis=0)
            m_new = l_sum = acc = None
            for off, n in chunks:
                cached = off < past
                kv0 = k0 + off - past
                if do_value:
                    vtc = cv_ref[...].T.astype(BF16) if cached else vt_ref[:, kv0:kv0 + n]
                    p = jnp.exp2(s_ref[s0 + off:s0 + off + n, c0:c0 + cw] - m_old[u])
                    ps = jnp.sum(p, axis=0, keepdims=True)
                    l_sum = ps if l_sum is None else l_sum + ps
                    pv = jnp.dot(vtc, p.astype(BF16), preferred_element_type=F32)
                    acc = pv if acc is None else acc + pv
                if do_score:
                    kc = ck_ref[...].astype(BF16) if cached else k_ref[kv0:kv0 + n, :]
                    st = lax.dot_general(kc, qs[c0:c0 + cw, :], nt, preferred_element_type=F32)
                    s_ref[s0 + off:s0 + off + n, c0:c0 + cw] = st
                    cm = jnp.max(st, axis=0, keepdims=True)
                    m_new = cm if m_new is None else jnp.maximum(m_new, cm)
            if do_score:
                m_ref[:, b * cols + c0:b * cols + c0 + cw] = m_new
            if do_value:
                out_t = acc / l_sum
                for g in range(GROUPS):
                    lo, hi = max(c0, g * tq), min(c0 + cw, (g + 1) * tq)
                    if lo < hi:
                        o_ref[q0 + lo - g * tq:q0 + hi - g * tq, g * HEAD_DIM:(g + 1) * HEAD_DIM] = (
                            out_t[:, lo - c0:hi - c0].T.astype(BF16))

    @pl.when(t == 0)
    def _():
        step(True, False)

    @pl.when((t > 0) & (t < n_tiles))
    def _():
        step(True, True)

    @pl.when(t == n_tiles)
    def _():
        step(False, True)


def _attention(q, k, vt, cache, n_batch, seq_len, tq, tk, nb):
    m = q.shape[0]
    nq = seq_len // tq
    has_cache = cache is not None
    assert nb == 1 or (nq == 1 and not has_cache and n_batch % nb == 0)
    n_tiles = (n_batch // nb) * N_KV_HEADS * nq

    def tile(u):
        return u // (N_KV_HEADS * nq), (u // nq) % N_KV_HEADS, u % nq

    def score_tile(t):
        return tile(jnp.minimum(t, n_tiles - 1))

    def value_tile(t):
        return tile(jnp.maximum(t - 1, 0))

    def q_map(t):
        b, h, i = score_tile(t)
        return b * nq + i, h

    def k_map(t):
        b, h, _ = score_tile(t)
        return b, h

    def ck_map(t):
        b, h, _ = score_tile(t)
        return b, 0, h

    def vt_map(t):
        b, h, _ = value_tile(t)
        return h, b

    def cv_map(t):
        b, h, _ = value_tile(t)
        return b, 0, h

    def o_map(t):
        b, h, i = value_tile(t)
        return b * nq + i, h

    in_specs = [
        pl.BlockSpec((nb * tq, GROUPS * HEAD_DIM), q_map),
        pl.BlockSpec((nb * seq_len, HEAD_DIM), k_map),
        pl.BlockSpec((HEAD_DIM, nb * seq_len), vt_map),
    ]
    args = [q, k, vt]
    past = 0
    if has_cache:
        past = cache[0].shape[1]
        in_specs += [pl.BlockSpec((None, past, HEAD_DIM), ck_map), pl.BlockSpec((None, past, HEAD_DIM), cv_map)]
        args += list(cache)
    cols = GROUPS * tq
    return pl.pallas_call(
        functools.partial(_attn_kernel, tq=tq, tk=tk, nb=nb, n_tiles=n_tiles, has_cache=has_cache),
        scratch_shapes=[pltpu.VMEM((nb * (past + seq_len), cols), F32), pltpu.VMEM((1, nb * cols), F32)],
        grid=(n_tiles + 1,),
        in_specs=in_specs,
        out_specs=pl.BlockSpec((nb * tq, GROUPS * HEAD_DIM), o_map),
        out_shape=jax.ShapeDtypeStruct((m, Q_WIDTH), BF16),
        compiler_params=_params(("arbitrary",), "attn"),
        name="attn_lat" if has_cache else "attn_ctx",
    )(*args)


def _merge_kernel(x_ref, sh_ref, sc_ref, gate_ref, g_ref, attn_ref, cu_ref, cup_ref, cun_ref, bg_ref, cw_ref,
                  wg_ref, wao_ref, wco_ref, wo_ref, o_ref, h_ref, conv_ref, *, tm, seq_len):

    def conv_rows():
        rows = 32
        w0, w1, w2 = cw_ref[0:1, :], cw_ref[1:2, :], cw_ref[2:3, :]
        zeros = jnp.zeros((HALO, CONV_WIDTH), F32)
        if seq_len > tm:
            tiles_per_seq = seq_len // tm
            tile_in_seq = jnp.full((HALO, CONV_WIDTH), pl.program_id(0) % tiles_per_seq, jnp.int32)
            first_halo = jnp.where(tile_in_seq == 0, 0.0, cup_ref[...])
            last_halo = jnp.where(tile_in_seq == tiles_per_seq - 1, 0.0, cun_ref[...])
        else:
            first_halo = last_halo = zeros
        for r0 in range(0, tm, rows):
            end = r0 + rows
            cur = cu_ref[r0:end, :]
            if r0 == 0:
                prev = first_halo
            elif r0 % seq_len == 0:
                prev = zeros
            else:
                prev = cu_ref[r0 - HALO:r0, :]
            if end == tm:
                nxt = last_halo
            elif end % seq_len == 0:
                nxt = zeros
            else:
                nxt = cu_ref[end:end + HALO, :]
            ext = jnp.concatenate([prev, cur, nxt], axis=0)
            up = pltpu.roll(ext, 1, axis=0)[HALO:HALO + rows]
            dn = pltpu.roll(ext, rows + 2 * HALO - 1, axis=0)[HALO:HALO + rows]
            conv = up * w0 + cur * w1 + dn * w2
            conv_ref[r0:end, :] = (bg_ref[r0:end, :] * conv).astype(BF16)

    _modulated_norm_rows(h_ref, x_ref, g_ref, sc_ref, sh_ref, tm)
    conv_rows()
    tn = 512
    for c0 in range(0, D_MODEL, tn):
        ya = jnp.dot(attn_ref[...], wao_ref[:, c0:c0 + tn], preferred_element_type=F32)
        ga = jax.nn.sigmoid(jnp.dot(h_ref[...], wg_ref[:, c0:c0 + tn], preferred_element_type=F32))
        gs = jax.nn.sigmoid(jnp.dot(h_ref[...], wg_ref[:, D_MODEL + c0:D_MODEL + c0 + tn], preferred_element_type=F32))
        yc = jnp.dot(conv_ref[...], wco_ref[:, c0:c0 + tn], preferred_element_type=F32)
        merged = (ga * ya + gs * yc).astype(BF16)
        part = jnp.dot(merged, wo_ref[c0:c0 + tn, :], preferred_element_type=F32)
        if c0 == 0:
            o_ref[...] = part
        elif c0 + tn < D_MODEL:
            o_ref[...] += part
        else:
            o_ref[...] = x_ref[...] + gate_ref[...] * (o_ref[...] + part)


def _merge(x, m3, row_fn, norm_g, attn, cu, bg, conv_w, w_gate, wao, wco, wo, seq_len, tm):
    m = x.shape[0]
    row = lambda i: (i, 0)
    const = lambda i: (0, 0)
    resident = functools.partial(pl.BlockSpec, index_map=const, pipeline_mode=pl.Buffered(1))
    assert seq_len % tm == 0 or tm % seq_len == 0
    blocks_per_tile = tm // HALO
    last_halo = m // HALO - 1
    in_specs = [
        pl.BlockSpec((tm, D_MODEL), row),
        _mod_spec(0, row_fn),
        _mod_spec(1, row_fn),
        _mod_spec(2, row_fn),
        pl.BlockSpec((1, D_MODEL), const),
        pl.BlockSpec((tm, Q_WIDTH), row),
        pl.BlockSpec((tm, CONV_WIDTH), row),
        pl.BlockSpec((HALO, CONV_WIDTH), lambda i: (jnp.maximum(i * blocks_per_tile - 1, 0), 0)),
        pl.BlockSpec((HALO, CONV_WIDTH), lambda i: (jnp.minimum((i + 1) * blocks_per_tile, last_halo), 0)),
        pl.BlockSpec((tm, CONV_WIDTH), row),
        pl.BlockSpec((3, CONV_WIDTH), const),
        resident((D_MODEL, 2 * D_MODEL)),
        resident((Q_WIDTH, D_MODEL)),
        resident((CONV_WIDTH, D_MODEL)),
        resident((D_MODEL, D_MODEL)),
    ]
    return pl.pallas_call(
        functools.partial(_merge_kernel, tm=tm, seq_len=seq_len),
        grid=(m // tm,),
        in_specs=in_specs,
        out_specs=pl.BlockSpec((tm, D_MODEL), row),
        out_shape=jax.ShapeDtypeStruct((m, D_MODEL), F32),
        scratch_shapes=[pltpu.VMEM((tm, D_MODEL), BF16), pltpu.VMEM((tm, CONV_WIDTH), BF16)],
        compiler_params=_params(("parallel",), "merge"),
        name="merge",
    )(x, m3, m3, m3, norm_g, attn, cu, cu, cu, bg, conv_w, w_gate, wao, wco, wo)


def _mlp_kernel(x_ref, sh_ref, sc_ref, gate_ref, g_ref, fg_ref, w1_ref, w2_ref, o_ref, h_ref):
    fc = pl.program_id(1)
    last = pl.num_programs(1) - 1
    tm = x_ref.shape[0]

    def part(lo=0, hi=tm):
        f = jnp.maximum(jnp.dot(h_ref[lo:hi, :], w1_ref[...], preferred_element_type=F32), 0.0)
        return jnp.dot((f * f).astype(BF16), w2_ref[...], preferred_element_type=F32)

    halves = ((0, tm // 2), (tm // 2, tm))

    @pl.when(fc == 0)
    def _():
        _modulated_norm_rows(h_ref, x_ref, g_ref, sc_ref, sh_ref, tm)
        for lo, hi in halves:
            o_ref[lo:hi, :] = part(lo, hi)

    @pl.when((fc > 0) & (fc < last))
    def _():
        o_ref[...] += part()

    @pl.when(fc == last)
    def _():
        gate = gate_ref[...]
        fg = fg_ref[...]
        rows = 16
        for lo, hi in halves:
            o_ref[lo:hi, :] += part(lo, hi)
            for r0 in range(lo, hi, rows):
                y = x_ref[r0:r0 + rows, :] + gate * o_ref[r0:r0 + rows, :]
                ms = jnp.mean(y * y, axis=-1, keepdims=True)
                o_ref[r0:r0 + rows, :] = y * lax.rsqrt(ms + EPS) * fg


def _mlp(x, m3, row_fn, norm_g, final_g, w1, w2, tm, tf):
    m = x.shape[0]
    row = lambda i, j: (i, 0)
    const = lambda i, j: (0, 0)
    in_specs = [
        pl.BlockSpec((tm, D_MODEL), row),
        _mod_spec(3, row_fn),
        _mod_spec(4, row_fn),
        _mod_spec(5, row_fn),
        pl.BlockSpec((1, D_MODEL), const),
        pl.BlockSpec((1, D_MODEL), const),
        pl.BlockSpec((D_MODEL, tf), lambda i, j: (0, j)),
        pl.BlockSpec((tf, D_MODEL), lambda i, j: (j, 0)),
    ]
    return pl.pallas_call(
        _mlp_kernel,
        grid=(m // tm, D_FF // tf),
        in_specs=in_specs,
        out_specs=pl.BlockSpec((tm, D_MODEL), row),
        out_shape=jax.ShapeDtypeStruct((m, D_MODEL), F32),
        scratch_shapes=[pltpu.VMEM((tm, D_MODEL), BF16)],
        compiler_params=_params(("parallel", "arbitrary"), "mlp"),
        name="mlp",
    )(x, m3, m3, m3, norm_g, final_g, w1, w2)


def _rope_tables(n_tokens):
    rows = n_tokens // GRID_W
    row = jnp.broadcast_to(jnp.arange(rows)[:, None], (rows, GRID_W)).reshape(-1).astype(F32)
    col = jnp.broadcast_to(jnp.arange(GRID_W)[None, :], (rows, GRID_W)).reshape(-1).astype(F32)
    half = HEAD_DIM // 2
    inv = ROPE_THETA ** (-jnp.arange(0, half, 2, dtype=F32) / half)
    ang = jnp.concatenate([row[:, None] * inv, col[:, None] * inv], axis=-1)
    cos2 = jnp.repeat(jnp.cos(ang), 2, axis=-1)
    sign = jnp.tile(jnp.array([-1.0, 1.0], F32), half)
    sin2 = jnp.repeat(jnp.sin(ang), 2, axis=-1) * sign
    return cos2, sin2


def _project(x, m3, row_fn, seq_len, rope_tabs, emit_kv, lw, cast_jobs=()):
    xf = x.reshape(x.shape[0] * seq_len, D_MODEL)
    return _inproj(xf, m3, functools.partial(row_fn, tm=TILES["inproj"]), lw["norm1_g"], lw["w_proj"], lw["q_gain"],
                   lw["k_gain"], rope_tabs, seq_len, emit_kv, TILES["inproj"], cast_jobs)


def _finish(x, proj, m3, row_fn, seq_len, cache, emit_kv, lw):
    n_batch = x.shape[0]
    xf = x.reshape(n_batch * seq_len, D_MODEL)
    if emit_kv:
        q, k, vt, kf, vf, cu, bg = proj
    else:
        q, k, vt, cu, bg = proj
        kf = vf = None
    tq = min(TILES["attn_q"], seq_len)
    nb = TILES["attn_seqs"] if (tq == seq_len and cache is None) else 1
    attn = _attention(q, k, vt, cache, n_batch, seq_len, tq, min(TILES["attn_k"], seq_len), nb)
    x1 = _merge(xf, m3, functools.partial(row_fn, tm=TILES["merge"]), lw["norm1_g"], attn, cu, bg, lw["conv_w"],
                lw["w_gate"], lw["w_attn_out"], lw["w_conv_out"], lw["w_o"], seq_len, TILES["merge"])
    y = _mlp(x1, m3, functools.partial(row_fn, tm=TILES["mlp"]), lw["norm2_g"], lw["final_g"], lw["w_ff1"],
             lw["w_ff2"], TILES["mlp"], TILES["mlp_ff"])
    return y.reshape(x.shape), kf, vf


def _ctx_row(i, tm):
    return 0


def _lat_row(i, tm, seq_len):
    return 1 + (i * tm) // seq_len


def kernel(x_prompt, x_sample, cache_k, cache_v, c, c_ctx, w_mod, b_mod, norm1_g, norm2_g, w_in, q_gain, k_gain,
           conv_w, w_attn_out, w_conv_out, w_o, w_ff1, w_ff2, final_g):
    depth = w_mod.shape[0]
    batch, seq, _ = x_prompt.shape
    dec_batch, dec_seq, _ = x_sample.shape
    past = cache_k.shape[2]
    assert depth == 1 and 1 + dec_batch <= COND_ROWS

    cond = jnp.zeros((COND_ROWS, D_MODEL), F32).at[0].set(c_ctx).at[1:1 + dec_batch].set(c)
    rope_tabs = _rope_tables(dec_seq)

    xp, xs = x_prompt, x_sample
    new_ks, new_vs = [], []
    for l in range(depth):
        lw = {
            "norm1_g": norm1_g[l].reshape(1, D_MODEL),
            "norm2_g": norm2_g[l].reshape(1, D_MODEL),
            "final_g": final_g.reshape(1, D_MODEL),
            "q_gain": q_gain[l].reshape(1, HEAD_DIM),
            "k_gain": k_gain[l].reshape(1, HEAD_DIM),
            "conv_w": conv_w[l],
            "w_proj": w_in[l, :, :PROJ_WIDTH].astype(BF16),
        }
        m3 = _modulation(cond, w_mod[l], b_mod[l]).reshape(COND_ROWS, 1, N_MOD * D_MODEL)
        cache = (cache_k[:, l].reshape(dec_batch, past, KV_WIDTH), cache_v[:, l].reshape(dec_batch, past, KV_WIDTH))
        lat_row = functools.partial(_lat_row, seq_len=dec_seq)
        later = ("w_gate", "w_attn_out", "w_conv_out", "w_o", "w_ff1", "w_ff2")
        cast_jobs = ((w_in[l], 2 * D_MODEL), (w_attn_out[l], D_MODEL), (w_conv_out[l], D_MODEL), (w_o[l], D_MODEL),
                     (w_ff1[l], D_FF), (w_ff2[l], D_MODEL))
        lat_out = _project(xs, m3, lat_row, dec_seq, rope_tabs, False, lw, cast_jobs)
        lat_proj = lat_out[:len(lat_out) - len(later)]
        lw.update(zip(later, lat_out[len(lat_out) - len(later):]))
        ctx_proj = _project(xp, m3, _ctx_row, seq, None, True, lw)
        xp, kf, vf = _finish(xp, ctx_proj, m3, _ctx_row, seq, None, True, lw)
        xs, _, _ = _finish(xs, lat_proj, m3, lat_row, dec_seq, cache, False, lw)
        new_ks.append(kf.reshape(batch, seq, N_KV_HEADS, HEAD_DIM))
        new_vs.append(vf.reshape(batch, seq, N_KV_HEADS, HEAD_DIM))
    return xp, xs, jnp.stack(new_ks, axis=1), jnp.stack(new_vs, axis=1)
```
